```python
import math
import jax, jax.numpy as jnp
from jax import lax
import numpy as np

D_MODEL = 1024
BATCH = 8
SEQ = 4096
DEPTH = 4

N_META = 16
SSM_WIDTH = D_MODEL // 2
SSM_GROUP = 16
SSM_GROUPS = SSM_WIDTH // SSM_GROUP
SSM_STATE = 64
GLA_HEADS = 4
GLA_VDIM = D_MODEL // 2
GLA_KDIM = GLA_VDIM // 2
GLA_HK = GLA_KDIM // GLA_HEADS
GLA_HV = GLA_VDIM // GLA_HEADS
GLA_GATE_RANK = 16
GLA_GATE_TAU = 16.0
GLA_CHUNK = 64
D_FF = -(-8 * D_MODEL // (3 * 256)) * 256
EPS = 1e-6
SPLIT_SIZES = (SSM_WIDTH, GLA_KDIM, GLA_KDIM, GLA_VDIM, GLA_VDIM, GLA_GATE_RANK, D_MODEL, D_MODEL)
IN_COLS = SSM_WIDTH + 2 * GLA_KDIM + 2 * GLA_VDIM + GLA_GATE_RANK + 2 * D_MODEL

kernel_name = "hybrid_s5_gla_gated_trunk"


def rmsnorm(x, gain):
    xf = x.astype(jnp.float32)
    y = xf * lax.rsqrt(jnp.mean(xf * xf, axis=-1, keepdims=True) + EPS)
    return (y * gain.astype(jnp.float32)).astype(x.dtype)


def split_cols(z):
    idx = [int(i) for i in np.cumsum(SPLIT_SIZES)[:-1]]
    return jnp.split(z, idx, axis=-1)


def _cmul_scan_op(e1, e2):
    a1r, a1i, b1r, b1i = e1
    a2r, a2i, b2r, b2i = e2
    return (a2r * a1r - a2i * a1i,
            a2r * a1i + a2i * a1r,
            a2r * b1r - a2i * b1i + b2r,
            a2r * b1i + a2i * b1r + b2i)


def s5_branch(u, lam_re, lam_im, log_step, b_re, b_im, c_re, c_im, d_skip, w_glu, b_glu):
    f32 = jnp.float32
    bsz, t, _ = u.shape
    uf = u.astype(f32)
    step = jnp.exp(log_step.astype(f32))[:, None]
    lr = jnp.minimum(lam_re.astype(f32), -1e-4)
    li = lam_im.astype(f32)
    mag = jnp.exp(lr * step)
    ab_re = mag * jnp.cos(li * step)
    ab_im = mag * jnp.sin(li * step)
    den = lr * lr + li * li
    nr = ab_re - 1.0
    coef_re = (nr * lr + ab_im * li) / den
    coef_im = (ab_im * lr - nr * li) / den
    br, bi = b_re.astype(f32), b_im.astype(f32)
    bb_re = coef_re[..., None] * br - coef_im[..., None] * bi
    bb_im = coef_re[..., None] * bi + coef_im[..., None] * br
    ug = uf.reshape(bsz, t, SSM_GROUPS, SSM_GROUP)
    bu_re = jnp.einsum('btgh,gph->btgp', ug, bb_re)
    bu_im = jnp.einsum('btgh,gph->btgp', ug, bb_im)
    a_re = jnp.broadcast_to(ab_re, bu_re.shape)
    a_im = jnp.broadcast_to(ab_im, bu_re.shape)
    _, _, xs_re, xs_im = lax.associative_scan(_cmul_scan_op, (a_re, a_im, bu_re, bu_im), axis=1)
    y = (jnp.einsum('btgp,ghp->btgh', xs_re, c_re.astype(f32))
         - jnp.einsum('btgp,ghp->btgh', xs_im, c_im.astype(f32)))
    y = y.reshape(bsz, t, SSM_WIDTH) + d_skip.astype(f32) * uf
    act = jax.nn.gelu(y)
    out = act * jax.nn.sigmoid(act @ w_glu.astype(f32) + b_glu.astype(f32))
    return out.astype(u.dtype)


def gla_branch(q, k, v, r, a_low, w_alpha, b_alpha, norm_gain):
    f32 = jnp.float32
    bsz, t, _ = q.shape
    log_a = jax.nn.log_sigmoid((a_low @ w_alpha + b_alpha).astype(f32)) / GLA_GATE_TAU
    pad = GLA_CHUNK - N_META

    def chunks(z, hd):
        z = jnp.pad(z.astype(f32), ((0, 0), (pad, 0), (0, 0)))
        n = z.shape[1] // GLA_CHUNK
        return z.reshape(bsz, n, GLA_CHUNK, GLA_HEADS, hd).transpose(0, 3, 1, 2, 4)

    qc = chunks(q, GLA_HK) * (GLA_HK ** -0.5)
    kc = chunks(k, GLA_HK)
    vc = chunks(v, GLA_HV)
    gc = chunks(log_a, GLA_HK)
    bcum = jnp.cumsum(gc, axis=3)
    b_last = bcum[:, :, :, -1:, :]
    q_dec = qc * jnp.exp(bcum)
    k_intra = kc * jnp.exp(-bcum)
    k_state = kc * jnp.exp(b_last - bcum)
    causal = jnp.tril(jnp.ones((GLA_CHUNK, GLA_CHUNK), dtype=bool))
    scores = jnp.where(causal, jnp.einsum('bhncd,bhnsd->bhncs', q_dec, k_intra), 0.0)
    o_intra = jnp.einsum('bhncs,bhnse->bhnce', scores, vc)
    kv = jnp.einsum('bhncd,bhnce->bhnde', k_state, vc)
    decay = jnp.exp(b_last[:, :, :, 0, :])

    def step(state, inp):
        dec, kv_n = inp
        return dec[..., None] * state + kv_n, state

    init = jnp.zeros((bsz, GLA_HEADS, GLA_HK, GLA_HV), f32)
    _, s_prev = lax.scan(step, init, (jnp.moveaxis(decay, 2, 0), jnp.moveaxis(kv, 2, 0)))
    s_prev = jnp.moveaxis(s_prev, 0, 2)
    o = o_intra + jnp.einsum('bhncd,bhnde->bhnce', q_dec, s_prev)
    o = o.transpose(0, 2, 3, 1, 4).reshape(bsz, -1, GLA_HEADS, GLA_HV)[:, pad:]
    o = o * lax.rsqrt(jnp.mean(o * o, axis=-1, keepdims=True) + EPS)
    o = o.reshape(bsz, t, GLA_VDIM) * norm_gain.astype(f32)
    return (jax.nn.silu(r.astype(f32)) * o).astype(q.dtype)


def setup_inputs(seed: int = 0) -> dict:
    key = jax.random.key(seed)
    ks = jax.random.split(key, 28)
    nrm = jax.random.normal
    L, G, P, H = DEPTH, SSM_GROUPS, SSM_STATE, SSM_GROUP
    x = nrm(ks[0], (BATCH, SEQ, D_MODEL), jnp.float32)
    meta = nrm(ks[1], (N_META, D_MODEL), jnp.float32)
    norm1 = 1.0 + 0.02 * nrm(ks[2], (L, D_MODEL), jnp.float32)
    w_in = nrm(ks[3], (L, D_MODEL, IN_COLS), jnp.float32) * D_MODEL ** -0.5
    lam_re = -0.5 + 0.01 * nrm(ks[4], (L, G, P), jnp.float32)
    lam_im = math.pi * jnp.arange(P, dtype=jnp.float32) + 0.01 * nrm(ks[5], (L, G, P), jnp.float32)
    log_step = jax.random.uniform(ks[6], (L, G), jnp.float32, math.log(1e-3), math.log(1e-1))
    b_re = nrm(ks[7], (L, G, P, H), jnp.float32) * (2 * H) ** -0.5
    b_im = nrm(ks[8], (L, G, P, H), jnp.float32) * (2 * H) ** -0.5
    c_re = nrm(ks[9], (L, G, H, P), jnp.float32) * P ** -0.5
    c_im = nrm(ks[10], (L, G, H, P), jnp.float32) * P ** -0.5
    d_skip = nrm(ks[11], (L, SSM_WIDTH), jnp.float32)
    w_glu = nrm(ks[12], (L, SSM_WIDTH, SSM_WIDTH), jnp.float32) * SSM_WIDTH ** -0.5
    b_glu = 0.01 * nrm(ks[13], (L, SSM_WIDTH), jnp.float32)
    w_pa = nrm(ks[14], (L, SSM_WIDTH, D_MODEL), jnp.float32) * SSM_WIDTH ** -0.5
    w_alpha = nrm(ks[15], (L, GLA_GATE_RANK, GLA_KDIM), jnp.float32) * GLA_GATE_RANK ** -0.5
    b_alpha = 0.01 * nrm(ks[16], (L, GLA_KDIM), jnp.float32)
    gla_norm = 1.0 + 0.02 * nrm(ks[17], (L, GLA_VDIM), jnp.float32)
    w_pb = nrm(ks[18], (L, GLA_VDIM, D_MODEL), jnp.float32) * GLA_VDIM ** -0.5
    w_out = nrm(ks[19], (L, D_MODEL, D_MODEL), jnp.float32) * D_MODEL ** -0.5
    norm2 = 1.0 + 0.02 * nrm(ks[20], (L, D_MODEL), jnp.float32)
    w_ff1 = nrm(ks[21], (L, D_MODEL, D_FF), jnp.float32) * D_MODEL ** -0.5
    w_ff3 = nrm(ks[22], (L, D_MODEL, D_FF), jnp.float32) * D_MODEL ** -0.5
    w_ff2 = nrm(ks[23], (L, D_FF, D_MODEL), jnp.float32) * D_FF ** -0.5
    norm_f = 1.0 + 0.02 * nrm(ks[24], (D_MODEL,), jnp.float32)
    return {"x": x, "meta": meta, "norm1": norm1, "w_in": w_in, "lam_re": lam_re, "lam_im": lam_im,
            "log_step": log_step, "b_re": b_re, "b_im": b_im, "c_re": c_re, "c_im": c_im,
            "d_skip": d_skip, "w_glu": w_glu, "b_glu": b_glu, "w_pa": w_pa, "w_alpha": w_alpha,
            "b_alpha": b_alpha, "gla_norm": gla_norm, "w_pb": w_pb, "w_out": w_out, "norm2": norm2,
            "w_ff1": w_ff1, "w_ff3": w_ff3, "w_ff2": w_ff2, "norm_f": norm_f}


def reference(x, meta, norm1, w_in, lam_re, lam_im, log_step, b_re, b_im, c_re, c_im, d_skip,
              w_glu, b_glu, w_pa, w_alpha, b_alpha, gla_norm, w_pb, w_out, norm2,
              w_ff1, w_ff3, w_ff2, norm_f):
    bsz = x.shape[0]
    meta_b = jnp.broadcast_to(meta.astype(x.dtype)[None], (bsz, N_META, D_MODEL))
    h = jnp.concatenate([meta_b, x], axis=1)
    for l in range(DEPTH):
        z = rmsnorm(h, norm1[l])
        u, q, k, v, r, a_low, g_a, g_b = split_cols(z @ w_in[l])
        y_a = s5_branch(u, lam_re[l], lam_im[l], log_step[l], b_re[l], b_im[l], c_re[l], c_im[l],
                        d_skip[l], w_glu[l], b_glu[l]) @ w_pa[l]
        y_b = gla_branch(q, k, v, r, a_low, w_alpha[l], b_alpha[l], gla_norm[l]) @ w_pb[l]
        mixed = jax.nn.sigmoid(g_a) * y_a + jax.nn.sigmoid(g_b) * y_b
        h = h + mixed @ w_out[l]
        z2 = rmsnorm(h, norm2[l])
        h = h + (jax.nn.silu(z2 @ w_ff1[l]) * (z2 @ w_ff3[l])) @ w_ff2[l]
    return rmsnorm(h, norm_f)[:, N_META:]
```

```python
import functools
import math

import jax
import jax.numpy as jnp
from jax import lax
from jax.experimental import pallas as pl
from jax.experimental.pallas import tpu as pltpu

F32 = jnp.float32
BF16 = jnp.bfloat16

EPS = 1e-6
N_META = 16
SSM_GROUP = 16
SSM_STATE = 64
SSM_CHUNK = 16
GLA_HEADS = 4
GLA_CHUNK = 64
GLA_GATE_RANK = 16
GLA_GATE_TAU = 16.0
LANES = 128
VMEM_LIMIT = 56 * 1024 * 1024


def _cparams(*sem):
    return pltpu.CompilerParams(dimension_semantics=sem, vmem_limit_bytes=VMEM_LIMIT)


def _const_spec(shape):
    nd = len(shape)
    return pl.BlockSpec(shape, lambda *_: (0,) * nd, pipeline_mode=pl.Buffered(1))


def _rms(x, gain):
    ms = jnp.mean(x * x, axis=-1, keepdims=True)
    return x * lax.rsqrt(ms + EPS) * gain


def _sigmoid(x):
    return 1.0 / (1.0 + jnp.exp(-x))


def _gelu_tanh(x):
    c = math.sqrt(2.0 / math.pi)
    return x * (0.5 * (1.0 + jnp.tanh(c * (x + 0.044715 * (x * x * x)))))


def _dot(a, b):
    return jnp.dot(a, b, preferred_element_type=F32)


def _dot_tl(a, b):
    return lax.dot_general(a, b, (((0,), (0,)), ((), ())), preferred_element_type=F32)


def _dot_tr(a, b):
    return lax.dot_general(a, b, (((1,), (1,)), ((), ())), preferred_element_type=F32)


def _split_bf16(x):
    hi = x.astype(BF16)
    lo = (x - hi.astype(F32)).astype(BF16)
    return hi, lo


def _s5_prep_kernel(ls_ref, lrr_ref, lir_ref, lrc_ref, lic_ref, br_ref, bi_ref, cr_ref, ci_ref,
                    k_ref, car_ref, cai_ref, wtr_ref, wti_ref):
    L = SSM_CHUNK
    step = jnp.exp(ls_ref[0])
    lr = jnp.minimum(lrr_ref[0], -1e-4)
    li = lir_ref[0]
    c_re, c_im = cr_ref[0], ci_ref[0]
    ca_re, ca_im = [], []
    for tau in range(L + 1):
        mag = jnp.exp(lr * step * float(tau))
        ang = li * step * float(tau)
        pr, pi = mag * jnp.cos(ang), mag * jnp.sin(ang)
        ca_re.append(c_re * pr - c_im * pi)
        ca_im.append(c_re * pi + c_im * pr)
        car_ref[0, tau] = ca_re[-1]
        cai_ref[0, tau] = ca_im[-1]
    lrc = jnp.minimum(lrc_ref[0], -1e-4)
    lic = lic_ref[0]
    mag1 = jnp.exp(lrc * step)
    ab_re, ab_im = mag1 * jnp.cos(lic * step), mag1 * jnp.sin(lic * step)
    den = lrc * lrc + lic * lic
    nr = ab_re - 1.0
    coef_re = (nr * lrc + ab_im * lic) / den
    coef_im = (ab_im * lrc - nr * lic) / den
    b_re, b_im = br_ref[0], bi_ref[0]
    bb_re = coef_re * b_re - coef_im * b_im
    bb_im = coef_re * b_im + coef_im * b_re
    st_re = jnp.concatenate(ca_re[:L], axis=0)
    st_im = jnp.concatenate(ca_im[:L], axis=0)
    hp = lax.Precision.HIGHEST
    k_ref[0] = (jnp.dot(st_re, bb_re, precision=hp, preferred_element_type=F32)
                - jnp.dot(st_im, bb_im, precision=hp, preferred_element_type=F32))
    for s in range(L):
        e = float(L - 1 - s)
        m = jnp.exp(lrc * step * e)
        a = lic * step * e
        pr, pi = m * jnp.cos(a), m * jnp.sin(a)
        wtr_ref[0, s] = pr * bb_re - pi * bb_im
        wti_ref[0, s] = pr * bb_im + pi * bb_re


def _s5_prep(lam_re, lam_im, log_step, b_re, b_im, c_re, c_im):
    nl, G, P = lam_re.shape
    H = b_re.shape[-1]
    L = SSM_CHUNK
    n = nl * G
    row = lambda a: a.reshape(n, 1, P)
    col = lambda a: a.reshape(n, P, 1)
    blk3 = lambda s: pl.BlockSpec((1,) + s, lambda i: (i, 0, 0))
    blk4 = lambda s: pl.BlockSpec((1,) + s, lambda i: (i, 0, 0, 0))
    k, car, cai, wtr, wti = pl.pallas_call(
        _s5_prep_kernel,
        grid=(n,),
        in_specs=[blk3((1, 1)), blk3((1, P)), blk3((1, P)), blk3((P, 1)), blk3((P, 1)),
                  blk3((P, H)), blk3((P, H)), blk3((H, P)), blk3((H, P))],
        out_specs=[blk3((L * H, H)), blk4((L + 1, H, P)), blk4((L + 1, H, P)),
                   blk4((L, P, H)), blk4((L, P, H))],
        out_shape=[jax.ShapeDtypeStruct((n, L * H, H), F32),
                   jax.ShapeDtypeStruct((n, L + 1, H, P), F32),
                   jax.ShapeDtypeStruct((n, L + 1, H, P), F32),
                   jax.ShapeDtypeStruct((n, L, P, H), F32),
                   jax.ShapeDtypeStruct((n, L, P, H), F32)],
        compiler_params=_cparams("arbitrary"),
        name="s5_prep",
    )(log_step.reshape(n, 1, 1), row(lam_re), row(lam_im), col(lam_re), col(lam_im),
      b_re.reshape(n, P, H), b_im.reshape(n, P, H), c_re.reshape(n, H, P), c_im.reshape(n, H, P))

    tl = jnp.arange(L)
    lag = tl[None, :] - tl[:, None]
    kk = k.reshape(n, L, H, H)
    toep = jnp.where((lag >= 0)[None, :, :, None, None], kk[:, jnp.maximum(lag, 0)], 0.0)
    toep = toep.transpose(0, 1, 4, 2, 3).reshape(n, L * H, L * H).astype(BF16)
    win_re = wtr.transpose(0, 1, 3, 2).reshape(n, L * H, P)
    win_im = wti.transpose(0, 1, 3, 2).reshape(n, L * H, P)
    wout_re = car[:, 1:].transpose(0, 3, 1, 2).reshape(n, P, L * H)
    wout_im = (-cai[:, 1:]).transpose(0, 3, 1, 2).reshape(n, P, L * H)

    def pair_rows(w):
        w = w.reshape(n // 2, 2, w.shape[1], w.shape[2])
        z = jnp.zeros_like(w[:, 0])
        top = jnp.concatenate([w[:, 0], z], axis=2)
        bot = jnp.concatenate([z, w[:, 1]], axis=2)
        return jnp.concatenate([top, bot], axis=1)

    return (toep.reshape(nl, G // 2, 2, L * H, L * H),
            pair_rows(win_re).astype(BF16).reshape(nl, G // 2, 2 * L * H, 2 * P),
            pair_rows(win_im).astype(BF16).reshape(nl, G // 2, 2 * L * H, 2 * P),
            pair_rows(wout_re).astype(BF16).reshape(nl, G // 2, 2 * P, 2 * L * H),
            pair_rows(wout_im).astype(BF16).reshape(nl, G // 2, 2 * P, 2 * L * H))


def _s5_adv_kernel(ls_ref, lr_ref, li_ref, o_ref):
    step = jnp.exp(ls_ref[...])
    lr = jnp.minimum(lr_ref[...], -1e-4)
    e = float(SSM_CHUNK)
    mag = jnp.exp(lr * step * e)
    ang = li_ref[...] * step * e
    o_ref[0] = mag * jnp.cos(ang)
    o_ref[1] = mag * jnp.sin(ang)


def _s5_adv(lam_re, lam_im, log_step):
    nl, G, P = lam_re.shape
    n = nl * G
    out = pl.pallas_call(
        _s5_adv_kernel,
        out_shape=jax.ShapeDtypeStruct((2, n, P), F32),
        name="s5_adv",
    )(log_step.reshape(n, 1), lam_re.reshape(n, P), lam_im.reshape(n, P))
    return out.reshape(2, nl, G // 2, 2 * P).transpose(1, 2, 0, 3)


def _inproj_kernel(h_ref, gain_ref, w_ref, wal_ref, walpha_ref, balpha_ref, *out_refs, widths):
    la_ref = out_refs[-1]
    zb = _rms(h_ref[...], gain_ref[...]).astype(BF16)
    start = 0
    for ref, width in zip(out_refs[:-1], widths):
        ref[...] = _dot(zb, w_ref[:, start:start + width]).astype(ref.dtype)
        start += width
    a_low = _dot(zb, wal_ref[...])
    logits = _dot(a_low.astype(BF16), walpha_ref[...]) + balpha_ref[...]
    log_sig = jnp.minimum(logits, 0.0) - jnp.log(1.0 + jnp.exp(-jnp.abs(logits)))
    la_ref[...] = log_sig * (1.0 / GLA_GATE_TAU)


def _inproj(h, gain, w_main, w_al, w_alpha, b_alpha, widths, tm):
    n, d = h.shape
    kd = w_alpha.shape[1]
    row = lambda w: pl.BlockSpec((tm, w), lambda i: (i, 0))
    return pl.pallas_call(
        functools.partial(_inproj_kernel, widths=widths),
        grid=(n // tm,),
        in_specs=[row(d), _const_spec(gain.shape), _const_spec(w_main.shape), _const_spec(w_al.shape),
                  _const_spec(w_alpha.shape), _const_spec(b_alpha.shape)],
        out_specs=[row(w) for w in widths] + [row(kd)],
        out_shape=[jax.ShapeDtypeStruct((n, w), BF16) for w in widths]
                  + [jax.ShapeDtypeStruct((n, kd), F32)],
        compiler_params=_cparams("parallel"),
        name="inproj",
    )(h, gain, w_main, w_al, w_alpha, b_alpha)


def _s5_kernel(u_ref, toep_ref, wir_ref, wii_ref, wor_ref, woi_ref, adv_ref, d_ref, y_ref,
               vr_s, vi_s, xr_s, xi_s, *, n_chunks, bsz):
    u = u_ref[0]
    vr_s[...] = _dot(u, wir_ref[0])
    vi_s[...] = _dot(u, wii_ref[0])
    a_re = jnp.broadcast_to(adv_ref[0, 0:1, :], (bsz, adv_ref.shape[-1]))
    a_im = jnp.broadcast_to(adv_ref[0, 1:2, :], (bsz, adv_ref.shape[-1]))

    def step(c, carry):
        xr, xi = carry
        rows = pl.ds(pl.multiple_of(c * bsz, bsz), bsz)
        xr_s[rows, :] = xr
        xi_s[rows, :] = xi
        nr = a_re * xr - a_im * xi + vr_s[rows, :]
        ni = a_re * xi + a_im * xr + vi_s[rows, :]
        return nr, ni

    zero = jnp.zeros((bsz, adv_ref.shape[-1]), F32)
    lax.fori_loop(0, n_chunks, step, (zero, zero), unroll=4)

    half = u.shape[1] // 2
    y = jnp.concatenate([_dot(u[:, :half], toep_ref[0, 0]), _dot(u[:, half:], toep_ref[0, 1])], axis=1)
    y = y + _dot(xr_s[...].astype(BF16), wor_ref[0]) + _dot(xi_s[...].astype(BF16), woi_ref[0])
    y = y + u.astype(F32) * d_ref[0]
    y_ref[0] = y.astype(y_ref.dtype)


def _s5_scan(u_flat, toep, wir, wii, wor, woi, adv, dflat, n_chunks, bsz):
    npair, rows, width = u_flat.shape
    sp = adv.shape[-1]
    blk = lambda a: pl.BlockSpec((1,) + a.shape[1:], lambda i: (i,) + (0,) * (a.ndim - 1))
    return pl.pallas_call(
        functools.partial(_s5_kernel, n_chunks=n_chunks, bsz=bsz),
        grid=(npair,),
        in_specs=[blk(u_flat), blk(toep), blk(wir), blk(wii), blk(wor), blk(woi), blk(adv), blk(dflat)],
        out_specs=blk(u_flat),
        out_shape=jax.ShapeDtypeStruct(u_flat.shape, BF16),
        scratch_shapes=[pltpu.VMEM((rows, sp), F32)] * 4,
        compiler_params=_cparams("parallel"),
        name="s5_scan",
    )(u_flat, toep, wir, wii, wor, woi, adv, dflat)


def _gla_kernel(q_ref, k_ref, v_ref, la_ref, r_ref, gain_ref, o_ref, s_ref, *, n_pad, tile, hk, hv):
    i = pl.program_id(1)
    C = GLA_CHUNK

    @pl.when(i == 0)
    def _():
        s_ref[...] = jnp.zeros_like(s_ref)

    rr = lax.broadcasted_iota(jnp.int32, (C, C), 0)
    cc = lax.broadcasted_iota(jnp.int32, (C, C), 1)
    causal = rr >= cc
    tri = causal.astype(BF16)
    ones = jnp.ones((C, hv), BF16)
    scale = hk ** -0.5

    def chunk(c, carry):
        rows = pl.ds(pl.multiple_of(c * C, C), C)
        t = i * tile + c * C + lax.broadcasted_iota(jnp.int32, (C, 1), 0)
        g = jnp.where(t >= n_pad, la_ref[rows, :], 0.0)
        g_hi, g_lo = _split_bf16(g)
        bcum = _dot(tri, g_hi) + _dot(tri, g_lo)
        b_last = bcum[C - 1:C, :]
        q_dec = (q_ref[rows, :].astype(F32) * scale * jnp.exp(bcum)).astype(BF16)
        kf = k_ref[rows, :].astype(F32)
        k_intra = (kf * jnp.exp(-bcum)).astype(BF16)
        k_state = (kf * jnp.exp(b_last - bcum)).astype(BF16)
        for h in range(GLA_HEADS):
            ks = slice(h * hk, (h + 1) * hk)
            vs = slice(h * hv, (h + 1) * hv)
            vh = v_ref[rows, vs]
            s_old = s_ref[h]
            scores = jnp.where(causal, _dot_tr(q_dec[:, ks], k_intra[:, ks]), 0.0)
            o = _dot(scores.astype(BF16), vh) + _dot(q_dec[:, ks], s_old.astype(BF16))
            decay = jnp.exp(_dot_tl(g_hi[:, ks], ones) + _dot_tl(g_lo[:, ks], ones))
            s_ref[h] = decay * s_old + _dot_tl(k_state[:, ks], vh)
            on = _rms(o, gain_ref[:, vs])
            rg = r_ref[rows, vs].astype(F32)
            o_ref[rows, vs] = (rg * _sigmoid(rg) * on).astype(o_ref.dtype)
        return carry

    lax.fori_loop(0, tile // C, chunk, 0)


def _gla(q, k, v, la, r, gain, bsz, t_pad, n_pad, tile):
    n, kd = q.shape
    vd = v.shape[1]
    tiles = t_pad // tile
    hk, hv = kd // GLA_HEADS, vd // GLA_HEADS
    row = lambda w: pl.BlockSpec((tile, w), lambda b, i: (b * tiles + i, 0))
    return pl.pallas_call(
        functools.partial(_gla_kernel, n_pad=n_pad, tile=tile, hk=hk, hv=hv),
        grid=(bsz, tiles),
        in_specs=[row(kd), row(kd), row(vd), row(kd), row(vd),
                  pl.BlockSpec(gain.shape, lambda b, i: (0, 0))],
        out_specs=row(vd),
        out_shape=jax.ShapeDtypeStruct((n, vd), BF16),
        scratch_shapes=[pltpu.VMEM((GLA_HEADS, hk, hv), F32)],
        compiler_params=_cparams("parallel", "arbitrary"),
        name="gla",
    )(q, k, v, la, r, gain)


def _merge_ffn_kernel(h_ref, ys_ref, gl_ref, ga_ref, gb_ref, wglu_ref, bglu_ref, wpa_ref, wpb_ref,
                      wout_ref, g2_ref, w1_ref, w3_ref, w2_ref, gf_ref, o_ref, *, ff_chunk, final):
    act = _gelu_tanh(ys_ref[...].astype(F32))
    gate = _dot(act.astype(BF16), wglu_ref[...]) + bglu_ref[...]
    s5o = act * _sigmoid(gate)
    y_a = _dot(s5o.astype(BF16), wpa_ref[...])
    y_b = _dot(gl_ref[...], wpb_ref[...])
    mixed = _sigmoid(ga_ref[...].astype(F32)) * y_a + _sigmoid(gb_ref[...].astype(F32)) * y_b
    h1 = h_ref[...] + _dot(mixed.astype(BF16), wout_ref[...])
    z2 = _rms(h1, g2_ref[...]).astype(BF16)
    acc = h1
    d_ff = w1_ref.shape[1]
    for j in range(0, d_ff, ff_chunk):
        a = _dot(z2, w1_ref[:, j:j + ff_chunk])
        b = _dot(z2, w3_ref[:, j:j + ff_chunk])
        hid = (a * _sigmoid(a) * b).astype(BF16)
        acc = acc + _dot(hid, w2_ref[j:j + ff_chunk, :])
    if final:
        acc = _rms(acc, gf_ref[...])
    o_ref[...] = acc


def _merge_ffn(h, ys, gl, ga, gb, wglu, bglu, wpa, wpb, wout, g2, w1, w3, w2, gf, tm, ff_chunk, final):
    n, d = h.shape
    row = lambda w: pl.BlockSpec((tm, w), lambda i: (i, 0))
    consts = (wglu, bglu, wpa, wpb, wout, g2, w1, w3, w2, gf)
    return pl.pallas_call(
        functools.partial(_merge_ffn_kernel, ff_chunk=ff_chunk, final=final),
        grid=(n // tm,),
        in_specs=[row(d), row(ys.shape[1]), row(gl.shape[1]), row(d), row(d)]
                 + [_const_spec(c.shape) for c in consts],
        out_specs=row(d),
        out_shape=jax.ShapeDtypeStruct((n, d), F32),
        compiler_params=_cparams("parallel"),
        name="merge_ffn",
    )(h, ys, gl, ga, gb, *consts)


def _largest_divisor(n, cap, mult):
    best = mult
    for t in range(mult, cap + 1, mult):
        if n % t == 0:
            best = t
    return best


def kernel(x, meta, norm1, w_in, lam_re, lam_im, log_step, b_re, b_im, c_re, c_im, d_skip, w_glu, b_glu, w_pa, w_alpha, b_alpha, gla_norm, w_pb, w_out, norm2, w_ff1, w_ff3, w_ff2, norm_f):
    bsz, seq, d = x.shape
    depth = w_in.shape[0]
    G = lam_re.shape[1]
    sw = G * SSM_GROUP
    kd = w_alpha.shape[2]
    vd = w_pb.shape[1]
    d_ff = w_ff1.shape[2]
    L = SSM_CHUNK
    n_pad = (-(N_META + seq)) % GLA_CHUNK
    t_pad = n_pad + N_META + seq
    n = bsz * t_pad
    n_chunks = t_pad // L
    assert bsz % 8 == 0 and t_pad % L == 0 and G % 2 == 0

    tm = _largest_divisor(n, 512, 64)
    gla_tile = _largest_divisor(t_pad, 1024, GLA_CHUNK)
    ff_chunk = _largest_divisor(d_ff, 1408, LANES)

    widths = (sw, kd, kd, vd, vd, d, d)
    lo = sw + 2 * kd + 2 * vd
    w_main = jnp.concatenate([w_in[:, :, :lo], w_in[:, :, lo + GLA_GATE_RANK:]], axis=2).astype(BF16)
    w_al = jnp.pad(w_in[:, :, lo:lo + GLA_GATE_RANK], ((0, 0), (0, 0), (0, LANES - GLA_GATE_RANK))).astype(BF16)
    w_alpha_p = jnp.pad(w_alpha, ((0, 0), (0, LANES - GLA_GATE_RANK), (0, 0))).astype(BF16)
    bf = lambda a: a.astype(BF16)
    w_glu_b, w_pa_b, w_pb_b, w_out_b = bf(w_glu), bf(w_pa), bf(w_pb), bf(w_out)
    w1_b, w3_b, w2_b = bf(w_ff1), bf(w_ff3), bf(w_ff2)

    toep, wir, wii, wor, woi = _s5_prep(lam_re, lam_im, log_step, b_re, b_im, c_re, c_im)
    adv = _s5_adv(lam_re, lam_im, log_step)
    dflat = jnp.broadcast_to(d_skip.reshape(depth, G // 2, 2, 1, SSM_GROUP),
                             (depth, G // 2, 2, L, SSM_GROUP)).reshape(depth, G // 2, 1, 2 * L * SSM_GROUP)

    meta_b = jnp.broadcast_to(meta.astype(x.dtype)[None], (bsz, N_META, d))
    h = jnp.concatenate([jnp.zeros((bsz, n_pad, d), x.dtype), meta_b, x], axis=1).reshape(n, d)

    for l in range(depth):
        u, q, k, v, r, ga, gb, la = _inproj(
            h, norm1[l][None], w_main[l], w_al[l], w_alpha_p[l], b_alpha[l][None], widths, tm)
        u_flat = (u.reshape(bsz, n_chunks, L, G // 2, 2, SSM_GROUP)
                  .transpose(3, 1, 0, 4, 2, 5).reshape(G // 2, n_chunks * bsz, 2 * L * SSM_GROUP))
        y_flat = _s5_scan(u_flat, toep[l], wir[l], wii[l], wor[l], woi[l], adv[l], dflat[l], n_chunks, bsz)
        ys = (y_flat.reshape(G // 2, n_chunks, bsz, 2, L, SSM_GROUP)
              .transpose(2, 1, 4, 0, 3, 5).reshape(n, sw))
        gl = _gla(q, k, v, la, r, gla_norm[l][None], bsz, t_pad, n_pad, gla_tile)
        h = _merge_ffn(h, ys, gl, ga, gb, w_glu_b[l], b_glu[l][None], w_pa_b[l], w_pb_b[l], w_out_b[l],
                       norm2[l][None], w1_b[l], w3_b[l], w2_b[l], norm_f[None], tm, ff_chunk,
                       final=(l == depth - 1))
    return h.reshape(bsz, t_pad, d)[:, n_pad + N_META:]
```

```python
import functools
import math

import jax
import jax.numpy as jnp
from jax import lax
from jax.experimental import pallas as pl
from jax.experimental.pallas import tpu as pltpu

F32 = jnp.float32
BF16 = jnp.bfloat16

EPS = 1e-6
N_META = 16
SSM_GROUP = 16
SSM_CHUNK = 16
GLA_HEADS = 4
GLA_CHUNK = 64
GLA_GATE_RANK = 16
GLA_GATE_TAU = 16.0
LANES = 128
MXU_TILE = 256
BF16_SUBLANES = 16
SLAB_GROUPS = LANES // SSM_GROUP
TOK_PER_TILE = MXU_TILE // LANES
VMEM_LIMIT = 56 * 1024 * 1024


def _cparams(*sem):
    return pltpu.CompilerParams(dimension_semantics=sem, vmem_limit_bytes=VMEM_LIMIT)


def _const_spec(shape):
    nd = len(shape)
    return pl.BlockSpec(shape, lambda *_: (0,) * nd, pipeline_mode=pl.Buffered(1))


def _lead_spec(a):
    return pl.BlockSpec((1,) + a.shape[1:], lambda i, *_: (i,) + (0,) * (a.ndim - 1))


def _rms(x, gain):
    ms = jnp.mean(x * x, axis=-1, keepdims=True)
    return x * lax.rsqrt(ms + EPS) * gain


def _sigmoid(x):
    return 1.0 / (1.0 + jnp.exp(-x))


def _gelu_tanh(x):
    c = math.sqrt(2.0 / math.pi)
    return x * (0.5 * (1.0 + jnp.tanh(c * (x + 0.044715 * (x * x * x)))))


def _dot(a, b):
    return jnp.dot(a, b, preferred_element_type=F32)


def _dot_tl(a, b):
    return lax.dot_general(a, b, (((0,), (0,)), ((), ())), preferred_element_type=F32)


def _dot_tr(a, b):
    return lax.dot_general(a, b, (((1,), (1,)), ((), ())), preferred_element_type=F32)


def _split_bf16(x):
    hi = x.astype(BF16)
    lo = (x - hi.astype(F32)).astype(BF16)
    return hi, lo


def _s5_prep_kernel(ls_ref, lrr_ref, lir_ref, lrc_ref, lic_ref, br_ref, bi_ref, cr_ref, ci_ref,
                    k_ref, car_ref, cai_ref, wtr_ref, wti_ref):
    L = SSM_CHUNK
    G = cr_ref.shape[1]
    step = jnp.exp(ls_ref[0])
    lr = jnp.minimum(lrr_ref[0], -1e-4)
    li = lir_ref[0]
    c_re, c_im = cr_ref[0], ci_ref[0]
    ca_re, ca_im = [], []
    for tau in range(L + 1):
        mag = jnp.exp(lr * step * float(tau))
        ang = li * step * float(tau)
        pr, pi = mag * jnp.cos(ang), mag * jnp.sin(ang)
        ca_re.append(c_re * pr - c_im * pi)
        ca_im.append(c_re * pi + c_im * pr)
        car_ref[0, :, tau] = ca_re[-1]
        cai_ref[0, :, tau] = ca_im[-1]
    lrc = jnp.minimum(lrc_ref[0], -1e-4)
    lic = lic_ref[0]
    mag1 = jnp.exp(lrc * step)
    ab_re, ab_im = mag1 * jnp.cos(lic * step), mag1 * jnp.sin(lic * step)
    den = lrc * lrc + lic * lic
    nr = ab_re - 1.0
    coef_re = (nr * lrc + ab_im * lic) / den
    coef_im = (ab_im * lrc - nr * lic) / den
    b_re, b_im = br_ref[0], bi_ref[0]
    bb_re = coef_re * b_re - coef_im * b_im
    bb_im = coef_re * b_im + coef_im * b_re
    hp = lax.Precision.HIGHEST
    for g in range(G):
        st_re = jnp.concatenate([c[g] for c in ca_re[:L]], axis=0)
        st_im = jnp.concatenate([c[g] for c in ca_im[:L]], axis=0)
        k_ref[0, g] = (jnp.dot(st_re, bb_re[g], precision=hp, preferred_element_type=F32)
                       - jnp.dot(st_im, bb_im[g], precision=hp, preferred_element_type=F32))
    e = (L - 1) - lax.broadcasted_iota(jnp.int32, (1, 1, L), 2).astype(F32)
    mag = jnp.exp(lrc * step * e)
    ang = lic * step * e
    pr, pi = mag * jnp.cos(ang), mag * jnp.sin(ang)
    for s in range(L):
        wtr_ref[0, :, s] = pr[:, :, s:s + 1] * bb_re - pi[:, :, s:s + 1] * bb_im
        wti_ref[0, :, s] = pr[:, :, s:s + 1] * bb_im + pi[:, :, s:s + 1] * bb_re


def _s5_prep(lam_re, lam_im, log_step, b_re, b_im, c_re, c_im):
    nl, G, P = lam_re.shape
    H = b_re.shape[-1]
    L = SSM_CHUNK
    S = SLAB_GROUPS
    ins = (log_step.reshape(nl, G, 1, 1), lam_re.reshape(nl, G, 1, P), lam_im.reshape(nl, G, 1, P),
           lam_re.reshape(nl, G, P, 1), lam_im.reshape(nl, G, P, 1), b_re, b_im, c_re, c_im)
    outs = ((nl, G, L * H, H), (nl, G, L + 1, H, P), (nl, G, L + 1, H, P), (nl, G, L, P, H), (nl, G, L, P, H))
    slab = lambda shape: pl.BlockSpec((1, S) + shape[2:], lambda i, j: (i, j) + (0,) * (len(shape) - 2))
    k, car, cai, wtr, wti = pl.pallas_call(
        _s5_prep_kernel,
        grid=(nl, G // S),
        in_specs=[slab(a.shape) for a in ins],
        out_specs=[slab(o) for o in outs],
        out_shape=[jax.ShapeDtypeStruct(o, F32) for o in outs],
        compiler_params=_cparams("parallel", "parallel"),
        name="s5_prep",
    )(*ins)

    J = G // S
    nt = L // TOK_PER_TILE
    same = jnp.eye(S, dtype=bool)
    d_i = jnp.arange(nt)[:, None, None]
    s_i = jnp.arange(TOK_PER_TILE)[None, :, None]
    t_i = jnp.arange(TOK_PER_TILE)[None, None, :]
    lag = TOK_PER_TILE * d_i + t_i - s_i
    kk = k.reshape(nl, J, S, L, H, H)
    tt = jnp.where((lag >= 0)[None, None, None, :, :, :, None, None], kk[:, :, :, jnp.maximum(lag, 0)], 0.0)
    tt = tt.transpose(0, 1, 3, 4, 2, 7, 5, 6)
    tt = jnp.where(same[None, None, None, None, :, None, None, :, None], tt[..., None, :], 0.0)
    tstack = tt.reshape(nl, J, nt, MXU_TILE, MXU_TILE)[:, :, ::-1].reshape(nl, J, nt * MXU_TILE, MXU_TILE)

    def w_in(wt):
        w = wt.reshape(nl, J, S, L, P, H).transpose(0, 1, 3, 2, 5, 4)
        w = jnp.where(same[None, None, None, :, None, :, None], w[..., None, :], 0.0)
        return w.reshape(nl, J, L * S * H, S * P)

    def w_out(ca):
        w = ca[:, :, 1:].reshape(nl, J, S, L, H, P).transpose(0, 1, 2, 5, 3, 4)
        w = jnp.where(same[None, None, :, None, None, :, None], w[..., None, :], 0.0)
        return w.reshape(nl, J, S * P, L * S * H)

    win = jnp.concatenate([w_in(wtr), w_in(wti)], axis=3).astype(BF16)
    wout = jnp.concatenate([w_out(car), w_out(-cai)], axis=2).astype(BF16)
    return tstack.astype(BF16), win, wout


def _s5_adv_kernel(ls_ref, lr_ref, li_ref, o_ref):
    step = jnp.exp(ls_ref[...])
    lr = jnp.minimum(lr_ref[...], -1e-4)
    e = float(SSM_CHUNK)
    mag = jnp.exp(lr * step * e)
    ang = li_ref[...] * step * e
    o_ref[0] = mag * jnp.cos(ang)
    o_ref[1] = mag * jnp.sin(ang)


def _s5_adv(lam_re, lam_im, log_step):
    nl, G, P = lam_re.shape
    n = nl * G
    out = pl.pallas_call(
        _s5_adv_kernel,
        out_shape=jax.ShapeDtypeStruct((2, n, P), F32),
        name="s5_adv",
    )(log_step.reshape(n, 1), lam_re.reshape(n, P), lam_im.reshape(n, P))
    J = G // SLAB_GROUPS
    return out.reshape(2, nl, J, SLAB_GROUPS * P).transpose(1, 2, 0, 3)


def _inproj_kernel(h_ref, gain_ref, w_ref, wal_ref, walpha_ref, balpha_ref, u_ref, *out_refs, widths):
    la_ref = out_refs[-1]
    zb = _rms(h_ref[...], gain_ref[...]).astype(BF16)
    sw = u_ref.shape[0] * LANES
    u = _dot(zb, w_ref[:, :sw]).astype(u_ref.dtype)
    for j in range(u_ref.shape[0]):
        u_ref[j] = u[:, j * LANES:(j + 1) * LANES]
    start = sw
    for ref, width in zip(out_refs[:-1], widths):
        ref[...] = _dot(zb, w_ref[:, start:start + width]).astype(ref.dtype)
        start += width
    a_low = _dot(zb, wal_ref[...])
    logits = _dot(a_low.astype(BF16), walpha_ref[...]) + balpha_ref[...]
    log_sig = jnp.minimum(logits, 0.0) - jnp.log(1.0 + jnp.exp(-jnp.abs(logits)))
    la_ref[...] = log_sig * (1.0 / GLA_GATE_TAU)


def _inproj(h, gain, w_main, w_al, w_alpha, b_alpha, sw, widths, tm):
    n, d = h.shape
    kd = w_alpha.shape[1]
    row = lambda w: pl.BlockSpec((tm, w), lambda i: (i, 0))
    nj = sw // LANES
    return pl.pallas_call(
        functools.partial(_inproj_kernel, widths=widths),
        grid=(n // tm,),
        in_specs=[row(d), _const_spec(gain.shape), _const_spec(w_main.shape), _const_spec(w_al.shape),
                  _const_spec(w_alpha.shape), _const_spec(b_alpha.shape)],
        out_specs=[pl.BlockSpec((nj, tm, LANES), lambda i: (0, i, 0))]
                  + [row(w) for w in widths] + [row(kd)],
        out_shape=[jax.ShapeDtypeStruct((nj, n, LANES), BF16)]
                  + [jax.ShapeDtypeStruct((n, w), BF16) for w in widths]
                  + [jax.ShapeDtypeStruct((n, kd), F32)],
        compiler_params=_cparams("parallel"),
        name="inproj",
    )(h, gain, w_main, w_al, w_alpha, b_alpha)


def _s5_state_kernel(u_ref, win_ref, adv_ref, x_ref, v_s, *, n_chunks, bsz):
    v = _dot(u_ref[0], win_ref[0])
    nv = v_s.shape[0]
    nc = nv // 2
    for p in range(nv):
        v_s[p] = v[:, p * LANES:(p + 1) * LANES]
    a_re = [jnp.broadcast_to(adv_ref[0, 0:1, p * LANES:(p + 1) * LANES], (bsz, LANES)) for p in range(nc)]
    a_im = [jnp.broadcast_to(adv_ref[0, 1:2, p * LANES:(p + 1) * LANES], (bsz, LANES)) for p in range(nc)]

    def step(c, carry):
        rows = pl.ds(c, bsz, stride=n_chunks)
        new = []
        for p in range(nc):
            xr, xi = carry[p], carry[nc + p]
            vr = v_s[p, rows, :]
            vi = v_s[nc + p, rows, :]
            v_s[p, rows, :] = xr
            v_s[nc + p, rows, :] = xi
            new.append((a_re[p] * xr - a_im[p] * xi + vr, a_re[p] * xi + a_im[p] * xr + vi))
        return tuple(n[0] for n in new) + tuple(n[1] for n in new)

    zero = jnp.zeros((bsz, LANES), F32)
    lax.fori_loop(0, n_chunks, step, (zero,) * nv)
    for p in range(nv):
        x_ref[0, :, p * LANES:(p + 1) * LANES] = v_s[p].astype(x_ref.dtype)


def _s5_state(u_rows, win, adv, n_chunks, bsz):
    nj, rows, _ = u_rows.shape
    sw2 = win.shape[-1]
    return pl.pallas_call(
        functools.partial(_s5_state_kernel, n_chunks=n_chunks, bsz=bsz),
        grid=(nj,),
        in_specs=[_lead_spec(u_rows), _lead_spec(win), _lead_spec(adv)],
        out_specs=pl.BlockSpec((1, rows, sw2), lambda i: (i, 0, 0)),
        out_shape=jax.ShapeDtypeStruct((nj, rows, sw2), BF16),
        scratch_shapes=[pltpu.VMEM((sw2 // LANES, rows, LANES), F32)],
        compiler_params=_cparams("parallel"),
        name="s5_state",
    )(u_rows, win, adv)


def _s5_out_kernel(u_ref, x_ref, ts_ref, wo_ref, d_ref, y_ref):
    u = u_ref[0]
    x = x_ref[0]
    nt = u.shape[1] // MXU_TILE
    for b in range(nt):
        cols = slice(b * MXU_TILE, (b + 1) * MXU_TILE)
        y = _dot(u[:, :(b + 1) * MXU_TILE], ts_ref[0, (nt - 1 - b) * MXU_TILE:, :])
        y = y + _dot(x, wo_ref[0, :, cols])
        y = y + u[:, cols].astype(F32) * d_ref[0, :, cols]
        y_ref[0, :, cols] = y.astype(y_ref.dtype)


def _s5_out(u_rows, x_rows, tstack, wout, dflat, tr):
    nj, rows, width = u_rows.shape
    sw2 = x_rows.shape[-1]
    lead2 = lambda a: pl.BlockSpec((1,) + a.shape[1:], lambda j, i: (j,) + (0,) * (a.ndim - 1))
    return pl.pallas_call(
        _s5_out_kernel,
        grid=(nj, rows // tr),
        in_specs=[pl.BlockSpec((1, tr, width), lambda j, i: (j, i, 0)),
                  pl.BlockSpec((1, tr, sw2), lambda j, i: (j, i, 0)),
                  lead2(tstack), lead2(wout), lead2(dflat)],
        out_specs=pl.BlockSpec((1, tr, width), lambda j, i: (j, i, 0)),
        out_shape=jax.ShapeDtypeStruct(u_rows.shape, BF16),
        compiler_params=_cparams("parallel", "parallel"),
        name="s5_out",
    )(u_rows, x_rows, tstack, wout, dflat)


def _gla_kernel(q_ref, k_ref, v_ref, la_ref, r_ref, gain_ref, o_ref, s_ref, *, n_pad, tile, hk, hv):
    i = pl.program_id(1)
    C = GLA_CHUNK

    @pl.when(i == 0)
    def _():
        s_ref[...] = jnp.zeros_like(s_ref)

    rr = lax.broadcasted_iota(jnp.int32, (C, C), 0)
    cc = lax.broadcasted_iota(jnp.int32, (C, C), 1)
    causal = rr >= cc
    tri = causal.astype(BF16)
    ones = jnp.ones((C, hv), BF16)
    scale = hk ** -0.5

    def chunk(c, carry):
        rows = pl.ds(pl.multiple_of(c * C, C), C)
        t = i * tile + c * C + lax.broadcasted_iota(jnp.int32, (C, 1), 0)
        g = jnp.where(t >= n_pad, la_ref[rows, :], 0.0)
        g_hi, g_lo = _split_bf16(g)
        bcum = _dot(tri, g_hi) + _dot(tri, g_lo)
        b_last = bcum[C - 1:C, :]
        q_dec = (q_ref[rows, :].astype(F32) * scale * jnp.exp(bcum)).astype(BF16)
        kf = k_ref[rows, :].astype(F32)
        k_intra = (kf * jnp.exp(-bcum)).astype(BF16)
        k_state = (kf * jnp.exp(b_last - bcum)).astype(BF16)
        for h in range(GLA_HEADS):
            ks = slice(h * hk, (h + 1) * hk)
            vs = slice(h * hv, (h + 1) * hv)
            vh = v_ref[rows, vs]
            s_old = s_ref[h]
            scores = jnp.where(causal, _dot_tr(q_dec[:, ks], k_intra[:, ks]), 0.0)
            o = _dot(scores.astype(BF16), vh) + _dot(q_dec[:, ks], s_old.astype(BF16))
            decay = jnp.exp(_dot_tl(g_hi[:, ks], ones) + _dot_tl(g_lo[:, ks], ones))
            s_ref[h] = decay * s_old + _dot_tl(k_state[:, ks], vh)
            on = _rms(o, gain_ref[:, vs])
            rg = r_ref[rows, vs].astype(F32)
            o_ref[rows, vs] = (rg * _sigmoid(rg) * on).astype(o_ref.dtype)
        return carry

    lax.fori_loop(0, tile // C, chunk, 0)


def _gla(q, k, v, la, r, gain, bsz, t_pad, n_pad, tile):
    n, kd = q.shape
    vd = v.shape[1]
    tiles = t_pad // tile
    hk, hv = kd // GLA_HEADS, vd // GLA_HEADS
    row = lambda w: pl.BlockSpec((tile, w), lambda b, i: (b * tiles + i, 0))
    return pl.pallas_call(
        functools.partial(_gla_kernel, n_pad=n_pad, tile=tile, hk=hk, hv=hv),
        grid=(bsz, tiles),
        in_specs=[row(kd), row(kd), row(vd), row(kd), row(vd),
                  pl.BlockSpec(gain.shape, lambda b, i: (0, 0))],
        out_specs=row(vd),
        out_shape=jax.ShapeDtypeStruct((n, vd), BF16),
        scratch_shapes=[pltpu.VMEM((GLA_HEADS, hk, hv), F32)],
        compiler_params=_cparams("parallel", "arbitrary"),
        name="gla",
    )(q, k, v, la, r, gain)


def _merge_ffn_kernel(h_ref, ys_ref, gl_ref, ga_ref, gb_ref, wglu_ref, bglu_ref, wpa_ref, wpb_ref,
                      wout_ref, g2_ref, w1_ref, w3_ref, w2_ref, gf_ref, o_ref, *, ff_chunk, final):
    ys = jnp.concatenate([ys_ref[j] for j in range(ys_ref.shape[0])], axis=1)
    act = _gelu_tanh(ys.astype(F32))
    gate = _dot(act.astype(BF16), wglu_ref[...]) + bglu_ref[...]
    s5o = act * _sigmoid(gate)
    y_a = _dot(s5o.astype(BF16), wpa_ref[...])
    y_b = _dot(gl_ref[...], wpb_ref[...])
    mixed = _sigmoid(ga_ref[...].astype(F32)) * y_a + _sigmoid(gb_ref[...].astype(F32)) * y_b
    h1 = h_ref[...] + _dot(mixed.astype(BF16), wout_ref[...])
    z2 = _rms(h1, g2_ref[...]).astype(BF16)
    acc = h1
    d_ff = w1_ref.shape[1]
    for j in range(0, d_ff, ff_chunk):
        a = _dot(z2, w1_ref[:, j:j + ff_chunk])
        b = _dot(z2, w3_ref[:, j:j + ff_chunk])
        hid = (a * _sigmoid(a) * b).astype(BF16)
        acc = acc + _dot(hid, w2_ref[j:j + ff_chunk, :])
    if final:
        acc = _rms(acc, gf_ref[...])
    o_ref[...] = acc


def _merge_ffn(h, ys, gl, ga, gb, wglu, bglu, wpa, wpb, wout, g2, w1, w3, w2, gf, tm, ff_chunk, final):
    n, d = h.shape
    row = lambda w: pl.BlockSpec((tm, w), lambda i: (i, 0))
    consts = (wglu, bglu, wpa, wpb, wout, g2, w1, w3, w2, gf)
    return pl.pallas_call(
        functools.partial(_merge_ffn_kernel, ff_chunk=ff_chunk, final=final),
        grid=(n // tm,),
        in_specs=[row(d), pl.BlockSpec((ys.shape[0], tm, LANES), lambda i: (0, i, 0)),
                  row(gl.shape[1]), row(d), row(d)]
                 + [_const_spec(c.shape) for c in consts],
        out_specs=row(d),
        out_shape=jax.ShapeDtypeStruct((n, d), F32),
        compiler_params=_cparams("parallel"),
        name="merge_ffn",
    )(h, ys, gl, ga, gb, *consts)


def _largest_divisor(n, cap, mult):
    best = mult
    for t in range(mult, cap + 1, mult):
        if n % t == 0:
            best = t
    return best


def kernel(x, meta, norm1, w_in, lam_re, lam_im, log_step, b_re, b_im, c_re, c_im, d_skip, w_glu, b_glu, w_pa, w_alpha, b_alpha, gla_norm, w_pb, w_out, norm2, w_ff1, w_ff3, w_ff2, norm_f):
    bsz, seq, d = x.shape
    depth = w_in.shape[0]
    G = lam_re.shape[1]
    sw = G * SSM_GROUP
    kd = w_alpha.shape[2]
    vd = w_pb.shape[1]
    d_ff = w_ff1.shape[2]
    L = SSM_CHUNK
    n_pad = (-(N_META + seq)) % GLA_CHUNK
    t_pad = n_pad + N_META + seq
    n = bsz * t_pad
    n_chunks = t_pad // L
    nj = sw // LANES
    assert bsz % 8 == 0 and t_pad % L == 0 and G % SLAB_GROUPS == 0

    tm = _largest_divisor(n, 512, 64)
    gla_tile = _largest_divisor(t_pad, 1024, GLA_CHUNK)
    ff_chunk = _largest_divisor(d_ff, 1408, LANES)
    s5_rows = _largest_divisor(bsz * n_chunks, 1040, BF16_SUBLANES)

    widths = (kd, kd, vd, vd, d, d)
    lo = sw + 2 * kd + 2 * vd
    w_main = jnp.concatenate([w_in[:, :, :lo], w_in[:, :, lo + GLA_GATE_RANK:]], axis=2).astype(BF16)
    w_al = jnp.pad(w_in[:, :, lo:lo + GLA_GATE_RANK], ((0, 0), (0, 0), (0, LANES - GLA_GATE_RANK))).astype(BF16)
    w_alpha_p = jnp.pad(w_alpha, ((0, 0), (0, LANES - GLA_GATE_RANK), (0, 0))).astype(BF16)
    bf = lambda a: a.astype(BF16)
    w_glu_b, w_pa_b, w_pb_b, w_out_b = bf(w_glu), bf(w_pa), bf(w_pb), bf(w_out)
    w1_b, w3_b, w2_b = bf(w_ff1), bf(w_ff3), bf(w_ff2)

    tstack, win, wout = _s5_prep(lam_re, lam_im, log_step, b_re, b_im, c_re, c_im)
    adv = _s5_adv(lam_re, lam_im, log_step)
    dflat = jnp.tile(d_skip.reshape(depth, nj, 1, LANES), (1, 1, 1, L))

    meta_b = jnp.broadcast_to(meta.astype(x.dtype)[None], (bsz, N_META, d))
    h = jnp.concatenate([jnp.zeros((bsz, n_pad, d), x.dtype), meta_b, x], axis=1).reshape(n, d)

    for l in range(depth):
        u, q, k, v, r, ga, gb, la = _inproj(
            h, norm1[l][None], w_main[l], w_al[l], w_alpha_p[l], b_alpha[l][None], sw, widths, tm)
        u_rows = u.reshape(nj, bsz * n_chunks, L * LANES)
        x_rows = _s5_state(u_rows, win[l], adv[l], n_chunks, bsz)
        ys = _s5_out(u_rows, x_rows, tstack[l], wout[l], dflat[l], s5_rows).reshape(nj, n, LANES)
        gl = _gla(q, k, v, la, r, gla_norm[l][None], bsz, t_pad, n_pad, gla_tile)
        h = _merge_ffn(h, ys, gl, ga, gb, w_glu_b[l], b_glu[l][None], w_pa_b[l], w_pb_b[l], w_out_b[l],
                       norm2[l][None], w1_b[l], w3_b[l], w2_b[l], norm_f[None], tm, ff_chunk,
                       final=(l == depth - 1))
    return h.reshape(bsz, t_pad, d)[:, n_pad + N_META:]
```

```python
import functools
import math

import jax
import jax.numpy as jnp
from jax import lax
from jax.experimental import pallas as pl
from jax.experimental.pallas import tpu as pltpu

F32 = jnp.float32
BF16 = jnp.bfloat16
HIGHEST = lax.Precision.HIGHEST

EPS = 1e-6
N_META = 16
SSM_GROUP = 16
SSM_CHUNK = 16
GLA_HEADS = 4
GLA_CHUNK = 64
GLA_GATE_RANK = 16
GLA_GATE_TAU = 16.0
LANES = 128
MXU_TILE = 256
BF16_SUBLANES = 16
SLAB_GROUPS = LANES // SSM_GROUP
VMEM_LIMIT = 56 * 1024 * 1024


def _cparams(*sem):
    return pltpu.CompilerParams(dimension_semantics=sem, vmem_limit_bytes=VMEM_LIMIT)


def _const_spec(shape):
    nd = len(shape)
    return pl.BlockSpec(shape, lambda *_: (0,) * nd, pipeline_mode=pl.Buffered(1))


def _lead_spec(a):
    return pl.BlockSpec((1,) + a.shape[1:], lambda i, *_: (i,) + (0,) * (a.ndim - 1))


def _rms(x, gain):
    ms = jnp.mean(x * x, axis=-1, keepdims=True)
    return x * lax.rsqrt(ms + EPS) * gain


def _sigmoid(x):
    return 1.0 / (1.0 + jnp.exp(-x))


def _gelu_tanh(x):
    c = math.sqrt(2.0 / math.pi)
    return x * (0.5 * (1.0 + jnp.tanh(c * (x + 0.044715 * (x * x * x)))))


def _dot(a, b, precision=None):
    return jnp.dot(a, b, precision=precision, preferred_element_type=F32)


def _dot_tl(a, b):
    return lax.dot_general(a, b, (((0,), (0,)), ((), ())), preferred_element_type=F32)


def _dot_tr(a, b, precision=None):
    return lax.dot_general(a, b, (((1,), (1,)), ((), ())), precision=precision, preferred_element_type=F32)


def _split_bf16(x):
    hi = x.astype(BF16)
    lo = (x - hi.astype(F32)).astype(BF16)
    return hi, lo


def _iota2(shape, axis):
    return lax.broadcasted_iota(jnp.int32, shape, axis)


def _lam_pow(lr, li, step, e):
    mag = jnp.exp(lr * step * e)
    ang = li * step * e
    return mag * jnp.cos(ang), mag * jnp.sin(ang)


def _s5_prep_kernel(rows_ref, lr_ref, li_ref, ls_ref, btr_ref, bti_ref, cr_ref, ci_ref,
                    ts_ref, wi_ref, wo_ref, adv_ref):
    L, H = SSM_CHUNK, SSM_GROUP
    GH, P = cr_ref.shape[2], cr_ref.shape[3]
    GP = rows_ref.shape[3]
    sh_h, sh_p = H.bit_length() - 1, P.bit_length() - 1
    step = jnp.exp(ls_ref[0, 0])
    lr = jnp.minimum(lr_ref[0, 0], -1e-4)
    li = li_ref[0, 0]
    ab_re, ab_im = _lam_pow(lr, li, step, 1.0)
    den = lr * lr + li * li
    nr = ab_re - 1.0
    coef_re = (nr * lr + ab_im * li) / den
    coef_im = (ab_im * lr - nr * li) / den
    bt_re, bt_im = btr_ref[0, 0], bti_ref[0, 0]
    bb_re = coef_re * bt_re - coef_im * bt_im
    bb_im = coef_re * bt_im + coef_im * bt_re
    c_re, c_im = cr_ref[0, 0], ci_ref[0, 0]
    ca_re, ca_im = [], []
    for tau in range(L + 1):
        pr, pi = _lam_pow(lr, li, step, float(tau))
        ca_re.append(c_re * pr - c_im * pi)
        ca_im.append(c_re * pi + c_im * pr)

    k_all = (_dot_tr(bb_re, jnp.concatenate(ca_re[:L], axis=0), HIGHEST)
             - _dot_tr(bb_im, jnp.concatenate(ca_im[:L], axis=0), HIGHEST))
    same = (_iota2(k_all.shape, 0) >> sh_h) == ((_iota2(k_all.shape, 1) & (GH - 1)) >> sh_h)
    k_all = jnp.where(same, k_all, 0.0).astype(ts_ref.dtype)
    k_lag = lambda lag: k_all[:, lag * GH:(lag + 1) * GH]
    nt = L // 2
    for d in range(nt):
        r0 = (nt - 1 - d) * 2 * GH
        ts_ref[0, 0, r0:r0 + GH, 0:GH] = k_lag(2 * d)
        ts_ref[0, 0, r0:r0 + GH, GH:2 * GH] = k_lag(2 * d + 1)
        ts_ref[0, 0, r0 + GH:r0 + 2 * GH, 0:GH] = k_lag(2 * d - 1) if d else jnp.zeros((GH, GH), ts_ref.dtype)
        ts_ref[0, 0, r0 + GH:r0 + 2 * GH, GH:2 * GH] = k_lag(2 * d)

    sel_t = ((_iota2((GP, P), 0) & (P - 1)) == _iota2((GP, P), 1)).astype(F32)
    w_re = _dot_tr(sel_t, jnp.concatenate(ca_re[1:], axis=0), HIGHEST)
    w_im = _dot_tr(sel_t, jnp.concatenate(ca_im[1:], axis=0), HIGHEST)
    same = (_iota2(w_re.shape, 0) >> sh_p) == ((_iota2(w_re.shape, 1) & (GH - 1)) >> sh_h)
    wo_ref[0, 0, 0:GP, :] = jnp.where(same, w_re, 0.0).astype(wo_ref.dtype)
    wo_ref[0, 0, GP:2 * GP, :] = jnp.where(same, -w_im, 0.0).astype(wo_ref.dtype)

    sel = ((_iota2((P, GP), 1) & (P - 1)) == _iota2((P, GP), 0)).astype(F32)
    same = (_iota2((GH, GP), 0) >> sh_h) == (_iota2((GH, GP), 1) >> sh_p)
    bbt_re = jnp.where(same, _dot(bb_re, sel, HIGHEST), 0.0)
    bbt_im = jnp.where(same, _dot(bb_im, sel, HIGHEST), 0.0)
    lr_gp = jnp.minimum(rows_ref[0, 0, 0:1, :], -1e-4)
    li_gp = rows_ref[0, 0, 1:2, :]
    step_gp = jnp.exp(rows_ref[0, 0, 2:3, :])
    for s in range(L):
        ar, ai = _lam_pow(lr_gp, li_gp, step_gp, float(L - 1 - s))
        wi_ref[0, 0, s * GH:(s + 1) * GH, 0:GP] = (bbt_re * ar - bbt_im * ai).astype(wi_ref.dtype)
        wi_ref[0, 0, s * GH:(s + 1) * GH, GP:2 * GP] = (bbt_re * ai + bbt_im * ar).astype(wi_ref.dtype)
    ar, ai = _lam_pow(lr_gp, li_gp, step_gp, float(L))
    adv_ref[0, 0, 0:1, :] = ar
    adv_ref[0, 0, 1:2, :] = ai


def _s5_prep(lam_re, lam_im, log_step, b_re, b_im, c_re, c_im):
    nl, G, P = lam_re.shape
    H = b_re.shape[-1]
    L, S = SSM_CHUNK, SLAB_GROUPS
    J = G // S
    GH, GP = S * H, S * P
    per_h = lambda a: jnp.repeat(a[:, :, None, :], H, axis=2).reshape(nl, J, GH, a.shape[-1])
    rows = jnp.stack([lam_re.reshape(nl, J, GP), lam_im.reshape(nl, J, GP),
                      jnp.repeat(log_step, P, axis=-1).reshape(nl, J, GP)], axis=2)
    ins = (rows, per_h(lam_re), per_h(lam_im), per_h(log_step[..., None]),
           b_re.transpose(0, 1, 3, 2).reshape(nl, J, GH, P), b_im.transpose(0, 1, 3, 2).reshape(nl, J, GH, P),
           c_re.reshape(nl, J, GH, P), c_im.reshape(nl, J, GH, P))
    outs = ((nl, J, L * GH, 2 * GH), (nl, J, L * GH, 2 * GP), (nl, J, 2 * GP, L * GH), (nl, J, 2, GP))
    dts = (BF16, BF16, BF16, F32)
    blk = lambda shape: pl.BlockSpec((1, 1) + shape[2:], lambda i, j: (i, j, 0, 0))
    return pl.pallas_call(
        _s5_prep_kernel,
        grid=(nl, J),
        in_specs=[blk(a.shape) for a in ins],
        out_specs=[blk(o) for o in outs],
        out_shape=[jax.ShapeDtypeStruct(o, dt) for o, dt in zip(outs, dts)],
        compiler_params=_cparams("parallel", "parallel"),
        name="s5_prep",
    )(*ins)


def _inproj_kernel(h_ref, gain_ref, w_ref, wal_ref, walpha_ref, balpha_ref, u_ref, *rest, widths):
    out_refs, u_s = rest[:-1], rest[-1]
    la_ref = out_refs[-1]
    L = SSM_CHUNK
    zb = _rms(h_ref[...], gain_ref[...]).astype(BF16)
    nj = u_ref.shape[0]
    u = _dot(zb, w_ref[:, :nj * LANES])
    for j in range(nj):
        u_s[j] = u[:, j * LANES:(j + 1) * LANES]
    rows = u_ref.shape[1]
    for j in range(nj):
        for t in range(L):
            u_ref[j, :, t * LANES:(t + 1) * LANES] = u_s[j, pl.ds(t, rows, stride=L), :].astype(u_ref.dtype)
    start = nj * LANES
    for ref, width in zip(out_refs[:-1], widths):
        ref[...] = _dot(zb, w_ref[:, start:start + width]).astype(ref.dtype)
        start += width
    a_low = _dot(zb, wal_ref[...])
    logits = _dot(a_low.astype(BF16), walpha_ref[...]) + balpha_ref[...]
    log_sig = jnp.minimum(logits, 0.0) - jnp.log(1.0 + jnp.exp(-jnp.abs(logits)))
    la_ref[...] = log_sig * (1.0 / GLA_GATE_TAU)


def _inproj(h, gain, w_main, w_al, w_alpha, b_alpha, sw, widths, tm):
    n, d = h.shape
    kd = w_alpha.shape[1]
    L = SSM_CHUNK
    row = lambda w: pl.BlockSpec((tm, w), lambda i: (i, 0))
    nj = sw // LANES
    return pl.pallas_call(
        functools.partial(_inproj_kernel, widths=widths),
        grid=(n // tm,),
        in_specs=[row(d), _const_spec(gain.shape), _const_spec(w_main.shape), _const_spec(w_al.shape),
                  _const_spec(w_alpha.shape), _const_spec(b_alpha.shape)],
        out_specs=[pl.BlockSpec((nj, tm // L, L * LANES), lambda i: (0, i, 0))]
                  + [row(w) for w in widths] + [row(kd)],
        out_shape=[jax.ShapeDtypeStruct((nj, n // L, L * LANES), BF16)]
                  + [jax.ShapeDtypeStruct((n, w), BF16) for w in widths]
                  + [jax.ShapeDtypeStruct((n, kd), F32)],
        scratch_shapes=[pltpu.VMEM((nj, tm, LANES), F32)],
        compiler_params=_cparams("parallel"),
        name="inproj",
    )(h, gain, w_main, w_al, w_alpha, b_alpha)


def _s5_state_kernel(u_ref, win_ref, adv_ref, x_ref, v_s, *, n_chunks, bsz):
    v = _dot(u_ref[0], win_ref[0])
    nv = v_s.shape[0]
    nc = nv // 2
    for p in range(nv):
        v_s[p] = v[:, p * LANES:(p + 1) * LANES]
    a_re = [jnp.broadcast_to(adv_ref[0, 0:1, p * LANES:(p + 1) * LANES], (bsz, LANES)) for p in range(nc)]
    a_im = [jnp.broadcast_to(adv_ref[0, 1:2, p * LANES:(p + 1) * LANES], (bsz, LANES)) for p in range(nc)]

    def step(c, carry):
        rows = pl.ds(c, bsz, stride=n_chunks)
        new = []
        for p in range(nc):
            xr, xi = carry[p], carry[nc + p]
            vr = v_s[p, rows, :]
            vi = v_s[nc + p, rows, :]
            v_s[p, rows, :] = xr
            v_s[nc + p, rows, :] = xi
            new.append((a_re[p] * xr - a_im[p] * xi + vr, a_re[p] * xi + a_im[p] * xr + vi))
        return tuple(n[0] for n in new) + tuple(n[1] for n in new)

    zero = jnp.zeros((bsz, LANES), F32)
    lax.fori_loop(0, n_chunks, step, (zero,) * nv)
    for p in range(nv):
        x_ref[0, :, p * LANES:(p + 1) * LANES] = v_s[p].astype(x_ref.dtype)


def _s5_state(u_rows, win, adv, n_chunks, bsz):
    nj, rows, _ = u_rows.shape
    sw2 = win.shape[-1]
    return pl.pallas_call(
        functools.partial(_s5_state_kernel, n_chunks=n_chunks, bsz=bsz),
        grid=(nj,),
        in_specs=[_lead_spec(u_rows), _lead_spec(win), _lead_spec(adv)],
        out_specs=pl.BlockSpec((1, rows, sw2), lambda i: (i, 0, 0)),
        out_shape=jax.ShapeDtypeStruct((nj, rows, sw2), BF16),
        scratch_shapes=[pltpu.VMEM((sw2 // LANES, rows, LANES), F32)],
        compiler_params=_cparams("parallel"),
        name="s5_state",
    )(u_rows, win, adv)


def _s5_out_kernel(u_ref, x_ref, ts_ref, wo_ref, d_ref, y_ref):
    u = u_ref[0]
    x = x_ref[0]
    nt = u.shape[1] // MXU_TILE
    for b in range(nt):
        cols = slice(b * MXU_TILE, (b + 1) * MXU_TILE)
        y = _dot(u[:, :(b + 1) * MXU_TILE], ts_ref[0, (nt - 1 - b) * MXU_TILE:, :])
        y = y + _dot(x, wo_ref[0, :, cols])
        y = y + u[:, cols].astype(F32) * d_ref[0, :, cols]
        y_ref[0, :, cols] = y.astype(y_ref.dtype)


def _s5_out(u_rows, x_rows, tstack, wout, dflat, tr):
    nj, rows, width = u_rows.shape
    sw2 = x_rows.shape[-1]
    lead2 = lambda a: pl.BlockSpec((1,) + a.shape[1:], lambda j, i: (j,) + (0,) * (a.ndim - 1))
    return pl.pallas_call(
        _s5_out_kernel,
        grid=(nj, rows // tr),
        in_specs=[pl.BlockSpec((1, tr, width), lambda j, i: (j, i, 0)),
                  pl.BlockSpec((1, tr, sw2), lambda j, i: (j, i, 0)),
                  lead2(tstack), lead2(wout), lead2(dflat)],
        out_specs=pl.BlockSpec((1, tr, width), lambda j, i: (j, i, 0)),
        out_shape=jax.ShapeDtypeStruct(u_rows.shape, BF16),
        compiler_params=_cparams("parallel", "parallel"),
        name="s5_out",
    )(u_rows, x_rows, tstack, wout, dflat)


def _gla_kernel(q_ref, k_ref, v_ref, la_ref, r_ref, gain_ref, o_ref, s_ref, *, n_pad, tile, hk, hv):
    i = pl.program_id(1)
    C = GLA_CHUNK

    @pl.when(i == 0)
    def _():
        s_ref[...] = jnp.zeros_like(s_ref)

    causal = _iota2((C, C), 0) >= _iota2((C, C), 1)
    tri = causal.astype(BF16)
    ones = jnp.ones((C, hv), BF16)
    scale = hk ** -0.5

    for c in range(tile // C):
        rows = pl.ds(c * C, C)
        t = i * tile + c * C + _iota2((C, 1), 0)
        g = jnp.where(t >= n_pad, la_ref[rows, :], 0.0)
        g_hi, g_lo = _split_bf16(g)
        bcum = _dot(tri, g_hi) + _dot(tri, g_lo)
        b_last = bcum[C - 1:C, :]
        decay = jnp.exp(_dot_tl(g_hi, ones) + _dot_tl(g_lo, ones))
        q_dec = (q_ref[rows, :].astype(F32) * scale * jnp.exp(bcum)).astype(BF16)
        kf = k_ref[rows, :].astype(F32)
        k_intra = (kf * jnp.exp(-bcum)).astype(BF16)
        k_state = (kf * jnp.exp(b_last - bcum)).astype(BF16)
        for h in range(GLA_HEADS):
            ks = slice(h * hk, (h + 1) * hk)
            vs = slice(h * hv, (h + 1) * hv)
            vh = v_ref[rows, vs]
            s_old = s_ref[h]
            scores = jnp.where(causal, _dot_tr(q_dec[:, ks], k_intra[:, ks]), 0.0)
            o = _dot(scores.astype(BF16), vh) + _dot(q_dec[:, ks], s_old.astype(BF16))
            s_ref[h] = decay[ks, :] * s_old + _dot_tl(k_state[:, ks], vh)
            on = _rms(o, gain_ref[:, vs])
            rg = r_ref[rows, vs].astype(F32)
            o_ref[rows, vs] = (rg * _sigmoid(rg) * on).astype(o_ref.dtype)


def _gla(q, k, v, la, r, gain, bsz, t_pad, n_pad, tile):
    n, kd = q.shape
    vd = v.shape[1]
    tiles = t_pad // tile
    hk, hv = kd // GLA_HEADS, vd // GLA_HEADS
    row = lambda w: pl.BlockSpec((tile, w), lambda b, i: (b * tiles + i, 0))
    return pl.pallas_call(
        functools.partial(_gla_kernel, n_pad=n_pad, tile=tile, hk=hk, hv=hv),
        grid=(bsz, tiles),
        in_specs=[row(kd), row(kd), row(vd), row(kd), row(vd),
                  pl.BlockSpec(gain.shape, lambda b, i: (0, 0))],
        out_specs=row(vd),
        out_shape=jax.ShapeDtypeStruct((n, vd), BF16),
        scratch_shapes=[pltpu.VMEM((GLA_HEADS, hk, hv), F32)],
        compiler_params=_cparams("parallel", "arbitrary"),
        name="gla",
    )(q, k, v, la, r, gain)


def _merge_ffn_kernel(h_ref, ys_ref, gl_ref, ga_ref, gb_ref, wglu_ref, bglu_ref, wpa_ref, wpb_ref,
                      wout_ref, g2_ref, w1_ref, w3_ref, w2_ref, gf_ref, o_ref, y_s, *, ff_chunk, final):
    L = SSM_CHUNK
    nj, rows = ys_ref.shape[0], ys_ref.shape[1]
    for j in range(nj):
        for t in range(L):
            y_s[j, pl.ds(t, rows, stride=L), :] = ys_ref[j, :, t * LANES:(t + 1) * LANES].astype(F32)
    ys = jnp.concatenate([y_s[j] for j in range(nj)], axis=1)
    act = _gelu_tanh(ys)
    gate = _dot(act.astype(BF16), wglu_ref[...]) + bglu_ref[...]
    s5o = act * _sigmoid(gate)
    y_a = _dot(s5o.astype(BF16), wpa_ref[...])
    y_b = _dot(gl_ref[...], wpb_ref[...])
    mixed = _sigmoid(ga_ref[...].astype(F32)) * y_a + _sigmoid(gb_ref[...].astype(F32)) * y_b
    h1 = h_ref[...] + _dot(mixed.astype(BF16), wout_ref[...])
    z2 = _rms(h1, g2_ref[...]).astype(BF16)
    acc = h1
    d_ff = w1_ref.shape[1]
    for j in range(0, d_ff, ff_chunk):
        a = _dot(z2, w1_ref[:, j:j + ff_chunk])
        b = _dot(z2, w3_ref[:, j:j + ff_chunk])
        hid = (a * _sigmoid(a) * b).astype(BF16)
        acc = acc + _dot(hid, w2_ref[j:j + ff_chunk, :])
    if final:
        acc = _rms(acc, gf_ref[...])
    o_ref[...] = acc


def _merge_ffn(h, ys, gl, ga, gb, wglu, bglu, wpa, wpb, wout, g2, w1, w3, w2, gf, tm, ff_chunk, final):
    n, d = h.shape
    L = SSM_CHUNK
    nj = ys.shape[0]
    row = lambda w: pl.BlockSpec((tm, w), lambda i: (i, 0))
    consts = (wglu, bglu, wpa, wpb, wout, g2, w1, w3, w2, gf)
    return pl.pallas_call(
        functools.partial(_merge_ffn_kernel, ff_chunk=ff_chunk, final=final),
        grid=(n // tm,),
        in_specs=[row(d), pl.BlockSpec((nj, tm // L, L * LANES), lambda i: (0, i, 0)),
                  row(gl.shape[1]), row(d), row(d)]
                 + [_const_spec(c.shape) for c in consts],
        out_specs=row(d),
        out_shape=jax.ShapeDtypeStruct((n, d), F32),
        scratch_shapes=[pltpu.VMEM((nj, tm, LANES), F32)],
        compiler_params=_cparams("parallel"),
        name="merge_ffn",
    )(h, ys, gl, ga, gb, *consts)


def _largest_divisor(n, cap, mult):
    best = mult
    for t in range(mult, cap + 1, mult):
        if n % t == 0:
            best = t
    return best


def kernel(x, meta, norm1, w_in, lam_re, lam_im, log_step, b_re, b_im, c_re, c_im, d_skip, w_glu, b_glu, w_pa, w_alpha, b_alpha, gla_norm, w_pb, w_out, norm2, w_ff1, w_ff3, w_ff2, norm_f):
    bsz, seq, d = x.shape
    depth = w_in.shape[0]
    G = lam_re.shape[1]
    sw = G * SSM_GROUP
    kd = w_alpha.shape[2]
    vd = w_pb.shape[1]
    d_ff = w_ff1.shape[2]
    L = SSM_CHUNK
    n_pad = (-(N_META + seq)) % GLA_CHUNK
    t_pad = n_pad + N_META + seq
    n = bsz * t_pad
    n_chunks = t_pad // L
    nj = sw // LANES
    assert bsz % 8 == 0 and t_pad % L == 0 and G % SLAB_GROUPS == 0

    tm = _largest_divisor(n, 512, L * BF16_SUBLANES)
    gla_tile = _largest_divisor(t_pad, 1024, GLA_CHUNK)
    ff_chunk = _largest_divisor(d_ff, 1408, LANES)
    s5_rows = _largest_divisor(bsz * n_chunks, 1040, BF16_SUBLANES)

    widths = (kd, kd, vd, vd, d, d)
    lo = sw + 2 * kd + 2 * vd
    w_main = jnp.concatenate([w_in[:, :, :lo], w_in[:, :, lo + GLA_GATE_RANK:]], axis=2).astype(BF16)
    w_al = jnp.pad(w_in[:, :, lo:lo + GLA_GATE_RANK], ((0, 0), (0, 0), (0, LANES - GLA_GATE_RANK))).astype(BF16)
    w_alpha_p = jnp.pad(w_alpha, ((0, 0), (0, LANES - GLA_GATE_RANK), (0, 0))).astype(BF16)
    bf = lambda a: a.astype(BF16)
    w_glu_b, w_pa_b, w_pb_b, w_out_b = bf(w_glu), bf(w_pa), bf(w_pb), bf(w_out)
    w1_b, w3_b, w2_b = bf(w_ff1), bf(w_ff3), bf(w_ff2)

    tstack, win, wout, adv = _s5_prep(lam_re, lam_im, log_step, b_re, b_im, c_re, c_im)
    dflat = jnp.tile(d_skip.reshape(depth, nj, 1, LANES), (1, 1, 1, L))

    meta_b = jnp.broadcast_to(meta.astype(x.dtype)[None], (bsz, N_META, d))
    h = jnp.concatenate([jnp.zeros((bsz, n_pad, d), x.dtype), meta_b, x], axis=1).reshape(n, d)

    for l in range(depth):
        u_rows, q, k, v, r, ga, gb, la = _inproj(
            h, norm1[l][None], w_main[l], w_al[l], w_alpha_p[l], b_alpha[l][None], sw, widths, tm)
        x_rows = _s5_state(u_rows, win[l], adv[l], n_chunks, bsz)
        y_rows = _s5_out(u_rows, x_rows, tstack[l], wout[l], dflat[l], s5_rows)
        gl = _gla(q, k, v, la, r, gla_norm[l][None], bsz, t_pad, n_pad, gla_tile)
        h = _merge_ffn(h, y_rows, gl, ga, gb, w_glu_b[l], b_glu[l][None], w_pa_b[l], w_pb_b[l], w_out_b[l],
                       norm2[l][None], w1_b[l], w3_b[l], w2_b[l], norm_f[None], tm, ff_chunk,
                       final=(l == depth - 1))
    return h.reshape(bsz, t_pad, d)[:, n_pad + N_META:]
```

```python
import functools
import math

import jax
import jax.numpy as jnp
from jax import lax
from jax.experimental import pallas as pl
from jax.experimental.pallas import tpu as pltpu

F32 = jnp.float32
BF16 = jnp.bfloat16

EPS = 1e-6
N_META = 16
SSM_GROUP = 16
SSM_CHUNK = 16
GLA_HEADS = 4
GLA_CHUNK = 64
GLA_GATE_RANK = 16
GLA_GATE_TAU = 16.0
LANES = 128
MXU_TILE = 256
BF16_SUBLANES = 16
SLAB_GROUPS = LANES // SSM_GROUP
VMEM_LIMIT = 56 * 1024 * 1024


def _cparams(*sem):
    return pltpu.CompilerParams(dimension_semantics=sem, vmem_limit_bytes=VMEM_LIMIT)


def _const_spec(shape):
    nd = len(shape)
    return pl.BlockSpec(shape, lambda *_: (0,) * nd, pipeline_mode=pl.Buffered(1))


def _lead_spec(a):
    return pl.BlockSpec((1,) + a.shape[1:], lambda i, *_: (i,) + (0,) * (a.ndim - 1))


def _rms(x, gain):
    ms = jnp.mean(x * x, axis=-1, keepdims=True)
    return x * lax.rsqrt(ms + EPS) * gain


def _sigmoid(x):
    return 1.0 / (1.0 + jnp.exp(-x))


def _gelu_tanh(x):
    c = math.sqrt(2.0 / math.pi)
    return x * (0.5 * (1.0 + jnp.tanh(c * (x + 0.044715 * (x * x * x)))))


def _dot(a, b):
    return jnp.dot(a, b, preferred_element_type=F32)


def _dot_tl(a, b):
    return lax.dot_general(a, b, (((0,), (0,)), ((), ())), preferred_element_type=F32)


def _dot_tr(a, b):
    return lax.dot_general(a, b, (((1,), (1,)), ((), ())), preferred_element_type=F32)


def _split_bf16(x):
    hi = x.astype(BF16)
    lo = (x - hi.astype(F32)).astype(BF16)
    return hi, lo


def _iota2(shape, axis):
    return lax.broadcasted_iota(jnp.int32, shape, axis)


def _lam_bar(lr, li, step):
    mag = jnp.exp(lr * step)
    return mag * jnp.cos(li * step), mag * jnp.sin(li * step)


def _cmul(ar, ai, br, bi):
    return ar * br - ai * bi, ar * bi + ai * br


def _dot_tr3(a, b):
    (a_hi, a_lo), (b_hi, b_lo) = _split_bf16(a), _split_bf16(b)
    return _dot_tr(a_hi, b_hi) + _dot_tr(a_hi, b_lo) + _dot_tr(a_lo, b_hi)


def _s5_prep_kernel(rows_ref, lr_ref, li_ref, ls_ref, btr_ref, bti_ref, cr_ref, ci_ref,
                    ts_ref, wi_ref, wo_ref, adv_ref):
    L, H = SSM_CHUNK, SSM_GROUP
    GH, P = cr_ref.shape[2], cr_ref.shape[3]
    GP = rows_ref.shape[3]
    sh_h, sh_p = H.bit_length() - 1, P.bit_length() - 1
    step = jnp.exp(ls_ref[0, 0])
    lr = jnp.minimum(lr_ref[0, 0], -1e-4)
    li = li_ref[0, 0]
    ab_re, ab_im = _lam_bar(lr, li, step)
    den = lr * lr + li * li
    nr = ab_re - 1.0
    coef_re = (nr * lr + ab_im * li) / den
    coef_im = (ab_im * lr - nr * li) / den
    bt_re, bt_im = btr_ref[0, 0], bti_ref[0, 0]
    bb_re = coef_re * bt_re - coef_im * bt_im
    bb_im = coef_re * bt_im + coef_im * bt_re
    c_re, c_im = cr_ref[0, 0], ci_ref[0, 0]
    ca_re, ca_im = [c_re], [c_im]
    for _ in range(L):
        nxt = _cmul(ca_re[-1], ca_im[-1], ab_re, ab_im)
        ca_re.append(nxt[0])
        ca_im.append(nxt[1])

    k_all = (_dot_tr3(bb_re, jnp.concatenate(ca_re[:L], axis=0))
             - _dot_tr3(bb_im, jnp.concatenate(ca_im[:L], axis=0)))
    same = (_iota2(k_all.shape, 0) >> sh_h) == ((_iota2(k_all.shape, 1) & (GH - 1)) >> sh_h)
    k_all = jnp.where(same, k_all, 0.0).astype(ts_ref.dtype)
    k_lag = lambda lag: k_all[:, lag * GH:(lag + 1) * GH]
    nt = L // 2
    for d in range(nt):
        r0 = (nt - 1 - d) * 2 * GH
        ts_ref[0, 0, r0:r0 + GH, 0:GH] = k_lag(2 * d)
        ts_ref[0, 0, r0:r0 + GH, GH:2 * GH] = k_lag(2 * d + 1)
        ts_ref[0, 0, r0 + GH:r0 + 2 * GH, 0:GH] = k_lag(2 * d - 1) if d else jnp.zeros((GH, GH), ts_ref.dtype)
        ts_ref[0, 0, r0 + GH:r0 + 2 * GH, GH:2 * GH] = k_lag(2 * d)

    sel_t = ((_iota2((GP, P), 0) & (P - 1)) == _iota2((GP, P), 1)).astype(BF16)
    move_t = lambda x: sum(_dot_tr(sel_t, part) for part in _split_bf16(x))
    w_re = move_t(jnp.concatenate(ca_re[1:], axis=0))
    w_im = move_t(jnp.concatenate(ca_im[1:], axis=0))
    same = (_iota2(w_re.shape, 0) >> sh_p) == ((_iota2(w_re.shape, 1) & (GH - 1)) >> sh_h)
    wo_ref[0, 0, 0:GP, :] = jnp.where(same, w_re, 0.0).astype(wo_ref.dtype)
    wo_ref[0, 0, GP:2 * GP, :] = jnp.where(same, -w_im, 0.0).astype(wo_ref.dtype)

    sel = ((_iota2((P, GP), 1) & (P - 1)) == _iota2((P, GP), 0)).astype(BF16)
    tile_lanes = lambda x: sum(_dot(part, sel) for part in _split_bf16(x))
    same = (_iota2((GH, GP), 0) >> sh_h) == (_iota2((GH, GP), 1) >> sh_p)
    bbt_re = jnp.where(same, tile_lanes(bb_re), 0.0)
    bbt_im = jnp.where(same, tile_lanes(bb_im), 0.0)
    lr_gp = jnp.minimum(rows_ref[0, 0, 0:1, :], -1e-4)
    a_re, a_im = _lam_bar(lr_gp, rows_ref[0, 0, 1:2, :], jnp.exp(rows_ref[0, 0, 2:3, :]))
    pw = [(jnp.ones_like(a_re), jnp.zeros_like(a_re))]
    for _ in range(L):
        pw.append(_cmul(pw[-1][0], pw[-1][1], a_re, a_im))
    for s in range(L):
        ar, ai = pw[L - 1 - s]
        wi_ref[0, 0, s * GH:(s + 1) * GH, 0:GP] = (bbt_re * ar - bbt_im * ai).astype(wi_ref.dtype)
        wi_ref[0, 0, s * GH:(s + 1) * GH, GP:2 * GP] = (bbt_re * ai + bbt_im * ar).astype(wi_ref.dtype)
    adv_ref[0, 0, 0:1, :] = pw[L][0]
    adv_ref[0, 0, 1:2, :] = pw[L][1]


def _s5_prep(lam_re, lam_im, log_step, b_re, b_im, c_re, c_im):
    nl, G, P = lam_re.shape
    H = b_re.shape[-1]
    L, S = SSM_CHUNK, SLAB_GROUPS
    J = G // S
    GH, GP = S * H, S * P
    per_h = lambda a: jnp.repeat(a[:, :, None, :], H, axis=2).reshape(nl, J, GH, a.shape[-1])
    rows = jnp.stack([lam_re.reshape(nl, J, GP), lam_im.reshape(nl, J, GP),
                      jnp.repeat(log_step, P, axis=-1).reshape(nl, J, GP)], axis=2)
    ins = (rows, per_h(lam_re), per_h(lam_im), per_h(log_step[..., None]),
           b_re.transpose(0, 1, 3, 2).reshape(nl, J, GH, P), b_im.transpose(0, 1, 3, 2).reshape(nl, J, GH, P),
           c_re.reshape(nl, J, GH, P), c_im.reshape(nl, J, GH, P))
    outs = ((nl, J, L * GH, 2 * GH), (nl, J, L * GH, 2 * GP), (nl, J, 2 * GP, L * GH), (nl, J, 2, GP))
    dts = (BF16, BF16, BF16, F32)
    blk = lambda shape: pl.BlockSpec((1, 1) + shape[2:], lambda i, j: (i, j, 0, 0))
    return pl.pallas_call(
        _s5_prep_kernel,
        grid=(nl, J),
        in_specs=[blk(a.shape) for a in ins],
        out_specs=[blk(o) for o in outs],
        out_shape=[jax.ShapeDtypeStruct(o, dt) for o, dt in zip(outs, dts)],
        compiler_params=_cparams("parallel", "parallel"),
        name="s5_prep",
    )(*ins)


def _inproj_kernel(h_ref, gain_ref, w_ref, wal_ref, walpha_ref, balpha_ref, u_ref, *rest, widths):
    out_refs, u_s = rest[:-1], rest[-1]
    la_ref = out_refs[-1]
    L = SSM_CHUNK
    zb = _rms(h_ref[...], gain_ref[...]).astype(BF16)
    a_low = _dot(zb, wal_ref[...])
    logits = _dot(a_low.astype(BF16), walpha_ref[...]) + balpha_ref[...]
    log_sig = jnp.minimum(logits, 0.0) - jnp.log(1.0 + jnp.exp(-jnp.abs(logits)))
    la_ref[...] = log_sig * (1.0 / GLA_GATE_TAU)
    nj = u_ref.shape[0]
    u = _dot(zb, w_ref[:, :nj * LANES])
    for j in range(nj):
        u_s[j] = u[:, j * LANES:(j + 1) * LANES]
    rows = u_ref.shape[1]
    for j in range(nj):
        for t in range(L):
            u_ref[j, :, t * LANES:(t + 1) * LANES] = u_s[j, pl.ds(t, rows, stride=L), :].astype(u_ref.dtype)
    start = nj * LANES
    for ref, width in zip(out_refs[:-1], widths):
        ref[...] = _dot(zb, w_ref[:, start:start + width]).astype(ref.dtype)
        start += width


def _inproj(h, gain, w_main, w_al, w_alpha, b_alpha, sw, widths, tm):
    n, d = h.shape
    kd = w_alpha.shape[1]
    L = SSM_CHUNK
    row = lambda w: pl.BlockSpec((tm, w), lambda i: (i, 0))
    nj = sw // LANES
    return pl.pallas_call(
        functools.partial(_inproj_kernel, widths=widths),
        grid=(n // tm,),
        in_specs=[row(d), _const_spec(gain.shape), _const_spec(w_main.shape), _const_spec(w_al.shape),
                  _const_spec(w_alpha.shape), _const_spec(b_alpha.shape)],
        out_specs=[pl.BlockSpec((nj, tm // L, L * LANES), lambda i: (0, i, 0))]
                  + [row(w) for w in widths] + [row(kd)],
        out_shape=[jax.ShapeDtypeStruct((nj, n // L, L * LANES), BF16)]
                  + [jax.ShapeDtypeStruct((n, w), BF16) for w in widths]
                  + [jax.ShapeDtypeStruct((n, kd), F32)],
        scratch_shapes=[pltpu.VMEM((nj, tm, LANES), F32)],
        compiler_params=_cparams("parallel"),
        name="inproj",
    )(h, gain, w_main, w_al, w_alpha, b_alpha)


def _s5_state_kernel(u_ref, win_ref, adv_ref, x_ref, v_s, *, n_chunks, bsz):
    v = _dot(u_ref[0], win_ref[0])
    nv = v_s.shape[0]
    nc = nv // 2
    for p in range(nv):
        v_s[p] = v[:, p * LANES:(p + 1) * LANES]
    a_re = [jnp.broadcast_to(adv_ref[0, 0:1, p * LANES:(p + 1) * LANES], (bsz, LANES)) for p in range(nc)]
    a_im = [jnp.broadcast_to(adv_ref[0, 1:2, p * LANES:(p + 1) * LANES], (bsz, LANES)) for p in range(nc)]

    def step(c, carry):
        rows = pl.ds(c, bsz, stride=n_chunks)
        new = []
        for p in range(nc):
            xr, xi = carry[p], carry[nc + p]
            vr = v_s[p, rows, :]
            vi = v_s[nc + p, rows, :]
            v_s[p, rows, :] = xr
            v_s[nc + p, rows, :] = xi
            new.append((a_re[p] * xr - a_im[p] * xi + vr, a_re[p] * xi + a_im[p] * xr + vi))
        return tuple(n[0] for n in new) + tuple(n[1] for n in new)

    zero = jnp.zeros((bsz, LANES), F32)
    lax.fori_loop(0, n_chunks, step, (zero,) * nv)
    for p in range(nv):
        x_ref[0, :, p * LANES:(p + 1) * LANES] = v_s[p].astype(x_ref.dtype)


def _s5_state(u_rows, win, adv, n_chunks, bsz):
    nj, rows, _ = u_rows.shape
    sw2 = win.shape[-1]
    return pl.pallas_call(
        functools.partial(_s5_state_kernel, n_chunks=n_chunks, bsz=bsz),
        grid=(nj,),
        in_specs=[_lead_spec(u_rows), _lead_spec(win), _lead_spec(adv)],
        out_specs=pl.BlockSpec((1, rows, sw2), lambda i: (i, 0, 0)),
        out_shape=jax.ShapeDtypeStruct((nj, rows, sw2), BF16),
        scratch_shapes=[pltpu.VMEM((sw2 // LANES, rows, LANES), F32)],
        compiler_params=_cparams("parallel"),
        name="s5_state",
    )(u_rows, win, adv)


def _s5_out_kernel(u_ref, x_ref, ts_ref, wo_ref, d_ref, y_ref):
    u = u_ref[0]
    x = x_ref[0]
    nt = u.shape[1] // MXU_TILE
    for b in range(nt):
        cols = slice(b * MXU_TILE, (b + 1) * MXU_TILE)
        y = _dot(u[:, :(b + 1) * MXU_TILE], ts_ref[0, (nt - 1 - b) * MXU_TILE:, :])
        y = y + _dot(x, wo_ref[0, :, cols])
        y = y + u[:, cols].astype(F32) * d_ref[0, :, cols]
        y_ref[0, :, cols] = y.astype(y_ref.dtype)


def _s5_out(u_rows, x_rows, tstack, wout, dflat, tr):
    nj, rows, width = u_rows.shape
    sw2 = x_rows.shape[-1]
    lead2 = lambda a: pl.BlockSpec((1,) + a.shape[1:], lambda j, i: (j,) + (0,) * (a.ndim - 1))
    return pl.pallas_call(
        _s5_out_kernel,
        grid=(nj, rows // tr),
        in_specs=[pl.BlockSpec((1, tr, width), lambda j, i: (j, i, 0)),
                  pl.BlockSpec((1, tr, sw2), lambda j, i: (j, i, 0)),
                  lead2(tstack), lead2(wout), lead2(dflat)],
        out_specs=pl.BlockSpec((1, tr, width), lambda j, i: (j, i, 0)),
        out_shape=jax.ShapeDtypeStruct(u_rows.shape, BF16),
        compiler_params=_cparams("parallel", "parallel"),
        name="s5_out",
    )(u_rows, x_rows, tstack, wout, dflat)


def _gla_kernel(q_ref, k_ref, v_ref, la_ref, r_ref, gain_ref, o_ref, s_ref, *, n_pad, tile, hk, hv):
    i = pl.program_id(1)
    C = GLA_CHUNK

    @pl.when(i == 0)
    def _():
        s_ref[...] = jnp.zeros_like(s_ref)

    nh = GLA_HEADS
    kd = nh * hk
    sh_c, sh_k = C.bit_length() - 1, hk.bit_length() - 1
    tri = (_iota2((C, C), 0) >= _iota2((C, C), 1)).astype(BF16)
    causal = (_iota2((nh * C, C), 0) & (C - 1)) >= _iota2((nh * C, C), 1)
    own_head = (_iota2((nh * C, kd), 0) >> sh_c) == (_iota2((nh * C, kd), 1) >> sh_k)
    lane_head = _iota2((hv, kd), 1) >> sh_k
    eye = (_iota2((hv, hv), 0) == _iota2((hv, hv), 1)).astype(BF16)
    scale = hk ** -0.5

    for c in range(tile // C):
        rows = pl.ds(c * C, C)
        t = i * tile + c * C + _iota2((C, 1), 0)
        g = jnp.where(t >= n_pad, la_ref[rows, :], 0.0)
        g_hi, g_lo = _split_bf16(g)
        bcum = _dot(tri, g_hi) + _dot(tri, g_lo)
        b_last = bcum[C - 1:C, :]
        q_dec = (q_ref[rows, :].astype(F32) * scale * jnp.exp(bcum)).astype(BF16)
        kf = k_ref[rows, :].astype(F32)
        k_intra = (kf * jnp.exp(-bcum)).astype(BF16)
        k_state = (kf * jnp.exp(b_last - bcum)).astype(BF16)
        v = v_ref[rows, :]
        s_old = s_ref[...]
        q_heads = jnp.where(own_head, jnp.concatenate([q_dec] * nh, axis=0), jnp.zeros((), BF16))
        scores = jnp.where(causal, _dot_tr(q_heads, k_intra), 0.0).astype(BF16)
        o_inter = _dot_tr(q_heads, s_old.astype(BF16))
        kv_t = jnp.zeros((hv, kd), F32)
        for h in range(nh):
            rs = slice(h * C, (h + 1) * C)
            vs = slice(h * hv, (h + 1) * hv)
            o = _dot(scores[rs, :], v[:, vs]) + o_inter[rs, :]
            on = _rms(o, gain_ref[:, vs])
            rg = r_ref[rows, vs].astype(F32)
            o_ref[rows, vs] = (rg * _sigmoid(rg) * on).astype(o_ref.dtype)
            v_t = _dot_tr(eye, v[:, vs]).astype(BF16)
            kv_t = jnp.where(lane_head == h, _dot(v_t, k_state), kv_t)
        s_ref[...] = s_old * jnp.exp(b_last) + kv_t


def _gla(q, k, v, la, r, gain, bsz, t_pad, n_pad, tile):
    n, kd = q.shape
    vd = v.shape[1]
    tiles = t_pad // tile
    hk, hv = kd // GLA_HEADS, vd // GLA_HEADS
    row = lambda w: pl.BlockSpec((tile, w), lambda b, i: (b * tiles + i, 0))
    return pl.pallas_call(
        functools.partial(_gla_kernel, n_pad=n_pad, tile=tile, hk=hk, hv=hv),
        grid=(bsz, tiles),
        in_specs=[row(kd), row(kd), row(vd), row(kd), row(vd),
                  pl.BlockSpec(gain.shape, lambda b, i: (0, 0))],
        out_specs=row(vd),
        out_shape=jax.ShapeDtypeStruct((n, vd), BF16),
        scratch_shapes=[pltpu.VMEM((hv, kd), F32)],
        compiler_params=_cparams("parallel", "arbitrary"),
        name="gla",
    )(q, k, v, la, r, gain)


def _merge_ffn_kernel(h_ref, ys_ref, gl_ref, ga_ref, gb_ref, wglu_ref, bglu_ref, wpa_ref, wpb_ref,
                      wout_ref, g2_ref, w1_ref, w3_ref, w2_ref, gf_ref, o_ref, y_s, *, ff_chunk, final):
    L = SSM_CHUNK
    nj, rows = ys_ref.shape[0], ys_ref.shape[1]
    for j in range(nj):
        for t in range(L):
            y_s[j, pl.ds(t, rows, stride=L), :] = ys_ref[j, :, t * LANES:(t + 1) * LANES].astype(F32)
    ys = jnp.concatenate([y_s[j] for j in range(nj)], axis=1)
    act = _gelu_tanh(ys)
    gate = _dot(act.astype(BF16), wglu_ref[...]) + bglu_ref[...]
    s5o = act * _sigmoid(gate)
    y_a = _dot(s5o.astype(BF16), wpa_ref[...])
    y_b = _dot(gl_ref[...], wpb_ref[...])
    mixed = _sigmoid(ga_ref[...].astype(F32)) * y_a + _sigmoid(gb_ref[...].astype(F32)) * y_b
    h1 = h_ref[...] + _dot(mixed.astype(BF16), wout_ref[...])
    z2 = _rms(h1, g2_ref[...]).astype(BF16)
    acc = h1
    d_ff = w1_ref.shape[1]
    for j in range(0, d_ff, ff_chunk):
        a = _dot(z2, w1_ref[:, j:j + ff_chunk])
        b = _dot(z2, w3_ref[:, j:j + ff_chunk])
        hid = (a * _sigmoid(a) * b).astype(BF16)
        acc = acc + _dot(hid, w2_ref[j:j + ff_chunk, :])
    if final:
        acc = _rms(acc, gf_ref[...])
    o_ref[...] = acc


def _merge_ffn(h, ys, gl, ga, gb, wglu, bglu, wpa, wpb, wout, g2, w1, w3, w2, gf, tm, ff_chunk, final):
    n, d = h.shape
    L = SSM_CHUNK
    nj = ys.shape[0]
    row = lambda w: pl.BlockSpec((tm, w), lambda i: (i, 0))
    consts = (wglu, bglu, wpa, wpb, wout, g2, w1, w3, w2, gf)
    return pl.pallas_call(
        functools.partial(_merge_ffn_kernel, ff_chunk=ff_chunk, final=final),
        grid=(n // tm,),
        in_specs=[row(d), pl.BlockSpec((nj, tm // L, L * LANES), lambda i: (0, i, 0)),
                  row(gl.shape[1]), row(d), row(d)]
                 + [_const_spec(c.shape) for c in consts],
        out_specs=row(d),
        out_shape=jax.ShapeDtypeStruct((n, d), F32),
        scratch_shapes=[pltpu.VMEM((nj, tm, LANES), F32)],
        compiler_params=_cparams("parallel"),
        name="merge_ffn",
    )(h, ys, gl, ga, gb, *consts)


def _largest_divisor(n, cap, mult):
    best = mult
    for t in range(mult, cap + 1, mult):
        if n % t == 0:
            best = t
    return best


def kernel(x, meta, norm1, w_in, lam_re, lam_im, log_step, b_re, b_im, c_re, c_im, d_skip, w_glu, b_glu, w_pa, w_alpha, b_alpha, gla_norm, w_pb, w_out, norm2, w_ff1, w_ff3, w_ff2, norm_f):
    bsz, seq, d = x.shape
    depth = w_in.shape[0]
    G = lam_re.shape[1]
    sw = G * SSM_GROUP
    kd = w_alpha.shape[2]
    vd = w_pb.shape[1]
    d_ff = w_ff1.shape[2]
    L = SSM_CHUNK
    n_pad = (-(N_META + seq)) % GLA_CHUNK
    t_pad = n_pad + N_META + seq
    n = bsz * t_pad
    n_chunks = t_pad // L
    nj = sw // LANES
    assert bsz % 8 == 0 and t_pad % L == 0 and G % SLAB_GROUPS == 0

    tm = _largest_divisor(n, 512, L * BF16_SUBLANES)
    gla_tile = _largest_divisor(t_pad, 1024, GLA_CHUNK)
    ff_chunk = _largest_divisor(d_ff, 1408, LANES)
    s5_rows = _largest_divisor(bsz * n_chunks, 1040, BF16_SUBLANES)

    widths = (kd, kd, vd, vd, d, d)
    lo = sw + 2 * kd + 2 * vd
    w_main = jnp.concatenate([w_in[:, :, :lo], w_in[:, :, lo + GLA_GATE_RANK:]], axis=2).astype(BF16)
    w_al = jnp.pad(w_in[:, :, lo:lo + GLA_GATE_RANK], ((0, 0), (0, 0), (0, LANES - GLA_GATE_RANK))).astype(BF16)
    w_alpha_p = jnp.pad(w_alpha, ((0, 0), (0, LANES - GLA_GATE_RANK), (0, 0))).astype(BF16)
    bf = lambda a: a.astype(BF16)
    w_glu_b, w_pa_b, w_pb_b, w_out_b = bf(w_glu), bf(w_pa), bf(w_pb), bf(w_out)
    w1_b, w3_b, w2_b = bf(w_ff1), bf(w_ff3), bf(w_ff2)

    tstack, win, wout, adv = _s5_prep(lam_re, lam_im, log_step, b_re, b_im, c_re, c_im)
    dflat = jnp.tile(d_skip.reshape(depth, nj, 1, LANES), (1, 1, 1, L))

    meta_b = jnp.broadcast_to(meta.astype(x.dtype)[None], (bsz, N_META, d))
    h = jnp.concatenate([jnp.zeros((bsz, n_pad, d), x.dtype), meta_b, x], axis=1).reshape(n, d)

    for l in range(depth):
        u_rows, q, k, v, r, ga, gb, la = _inproj(
            h, norm1[l][None], w_main[l], w_al[l], w_alpha_p[l], b_alpha[l][None], sw, widths, tm)
        x_rows = _s5_state(u_rows, win[l], adv[l], n_chunks, bsz)
        y_rows = _s5_out(u_rows, x_rows, tstack[l], wout[l], dflat[l], s5_rows)
        gl = _gla(q, k, v, la, r, gla_norm[l][None], bsz, t_pad, n_pad, gla_tile)
        h = _merge_ffn(h, y_rows, gl, ga, gb, w_glu_b[l], b_glu[l][None], w_pa_b[l], w_pb_b[l], w_out_b[l],
                       norm2[l][None], w1_b[l], w3_b[l], w2_b[l], norm_f[None], tm, ff_chunk,
                       final=(l == depth - 1))
    return h.reshape(bsz, t_pad, d)[:, n_pad + N_META:]
```

```python
import functools
import math

import jax
import jax.numpy as jnp
from jax import lax
from jax.experimental import pallas as pl
from jax.experimental.pallas import tpu as pltpu

F32 = jnp.float32
BF16 = jnp.bfloat16

EPS = 1e-6
N_META = 16
SSM_GROUP = 16
SSM_CHUNK = 16
GLA_HEADS = 4
GLA_CHUNK = 64
GLA_GATE_RANK = 16
GLA_GATE_TAU = 16.0
LANES = 128
MXU_TILE = 256
BF16_SUBLANES = 16
SLAB_GROUPS = LANES // SSM_GROUP
VMEM_LIMIT = 56 * 1024 * 1024


def _cparams(*sem):
    return pltpu.CompilerParams(dimension_semantics=sem, vmem_limit_bytes=VMEM_LIMIT)


def _const_spec(shape):
    nd = len(shape)
    return pl.BlockSpec(shape, lambda *_: (0,) * nd, pipeline_mode=pl.Buffered(1))


def _lead_spec(a):
    return pl.BlockSpec((1,) + a.shape[1:], lambda i, *_: (i,) + (0,) * (a.ndim - 1))


def _rms(x, gain):
    ms = jnp.mean(x * x, axis=-1, keepdims=True)
    return x * lax.rsqrt(ms + EPS) * gain


def _sigmoid(x):
    return 1.0 / (1.0 + jnp.exp(-x))


def _gelu_tanh(x):
    c = math.sqrt(2.0 / math.pi)
    return x * (0.5 * (1.0 + jnp.tanh(c * (x + 0.044715 * (x * x * x)))))


def _dot(a, b):
    return jnp.dot(a, b, preferred_element_type=F32)


def _dot_tl(a, b):
    return lax.dot_general(a, b, (((0,), (0,)), ((), ())), preferred_element_type=F32)


def _dot_tr(a, b):
    return lax.dot_general(a, b, (((1,), (1,)), ((), ())), preferred_element_type=F32)


def _split_bf16(x):
    hi = x.astype(BF16)
    lo = (x - hi.astype(F32)).astype(BF16)
    return hi, lo


def _iota2(shape, axis):
    return lax.broadcasted_iota(jnp.int32, shape, axis)


def _lam_bar(lr, li, step):
    mag = jnp.exp(lr * step)
    return mag * jnp.cos(li * step), mag * jnp.sin(li * step)


def _cmul(ar, ai, br, bi):
    return ar * br - ai * bi, ar * bi + ai * br


def _dot_tr3(a, b):
    (a_hi, a_lo), (b_hi, b_lo) = _split_bf16(a), _split_bf16(b)
    return _dot_tr(a_hi, b_hi) + _dot_tr(a_hi, b_lo) + _dot_tr(a_lo, b_hi)


def _s5_prep_kernel(rows_ref, lr_ref, li_ref, ls_ref, btr_ref, bti_ref, cr_ref, ci_ref,
                    ts_ref, wi_ref, wo_ref, adv_ref):
    L, H = SSM_CHUNK, SSM_GROUP
    GH, P = cr_ref.shape[2], cr_ref.shape[3]
    GP = rows_ref.shape[3]
    sh_h, sh_p = H.bit_length() - 1, P.bit_length() - 1
    step = jnp.exp(ls_ref[0, 0])
    lr = jnp.minimum(lr_ref[0, 0], -1e-4)
    li = li_ref[0, 0]
    ab_re, ab_im = _lam_bar(lr, li, step)
    den = lr * lr + li * li
    nr = ab_re - 1.0
    coef_re = (nr * lr + ab_im * li) / den
    coef_im = (ab_im * lr - nr * li) / den
    bt_re, bt_im = btr_ref[0, 0], bti_ref[0, 0]
    bb_re = coef_re * bt_re - coef_im * bt_im
    bb_im = coef_re * bt_im + coef_im * bt_re
    c_re, c_im = cr_ref[0, 0], ci_ref[0, 0]
    ca_re, ca_im = [c_re], [c_im]
    for _ in range(L):
        nxt = _cmul(ca_re[-1], ca_im[-1], ab_re, ab_im)
        ca_re.append(nxt[0])
        ca_im.append(nxt[1])

    k_all = (_dot_tr3(bb_re, jnp.concatenate(ca_re[:L], axis=0))
             - _dot_tr3(bb_im, jnp.concatenate(ca_im[:L], axis=0)))
    same = (_iota2(k_all.shape, 0) >> sh_h) == ((_iota2(k_all.shape, 1) & (GH - 1)) >> sh_h)
    k_all = jnp.where(same, k_all, 0.0).astype(ts_ref.dtype)
    k_lag = lambda lag: k_all[:, lag * GH:(lag + 1) * GH]
    nt = L // 2
    for d in range(nt):
        r0 = (nt - 1 - d) * 2 * GH
        ts_ref[0, 0, r0:r0 + GH, 0:GH] = k_lag(2 * d)
        ts_ref[0, 0, r0:r0 + GH, GH:2 * GH] = k_lag(2 * d + 1)
        ts_ref[0, 0, r0 + GH:r0 + 2 * GH, 0:GH] = k_lag(2 * d - 1) if d else jnp.zeros((GH, GH), ts_ref.dtype)
        ts_ref[0, 0, r0 + GH:r0 + 2 * GH, GH:2 * GH] = k_lag(2 * d)

    sel_t = ((_iota2((GP, P), 0) & (P - 1)) == _iota2((GP, P), 1)).astype(BF16)
    move_t = lambda x: sum(_dot_tr(sel_t, part) for part in _split_bf16(x))
    w_re = move_t(jnp.concatenate(ca_re[1:], axis=0))
    w_im = move_t(jnp.concatenate(ca_im[1:], axis=0))
    same = (_iota2(w_re.shape, 0) >> sh_p) == ((_iota2(w_re.shape, 1) & (GH - 1)) >> sh_h)
    wo_ref[0, 0, 0:GP, :] = jnp.where(same, w_re, 0.0).astype(wo_ref.dtype)
    wo_ref[0, 0, GP:2 * GP, :] = jnp.where(same, -w_im, 0.0).astype(wo_ref.dtype)

    sel = ((_iota2((P, GP), 1) & (P - 1)) == _iota2((P, GP), 0)).astype(BF16)
    tile_lanes = lambda x: sum(_dot(part, sel) for part in _split_bf16(x))
    same = (_iota2((GH, GP), 0) >> sh_h) == (_iota2((GH, GP), 1) >> sh_p)
    bbt_re = jnp.where(same, tile_lanes(bb_re), 0.0)
    bbt_im = jnp.where(same, tile_lanes(bb_im), 0.0)
    lr_gp = jnp.minimum(rows_ref[0, 0, 0:1, :], -1e-4)
    a_re, a_im = _lam_bar(lr_gp, rows_ref[0, 0, 1:2, :], jnp.exp(rows_ref[0, 0, 2:3, :]))
    pw = [(jnp.ones_like(a_re), jnp.zeros_like(a_re))]
    for _ in range(L):
        pw.append(_cmul(pw[-1][0], pw[-1][1], a_re, a_im))
    for s in range(L):
        ar, ai = pw[L - 1 - s]
        wi_ref[0, 0, s * GH:(s + 1) * GH, 0:GP] = (bbt_re * ar - bbt_im * ai).astype(wi_ref.dtype)
        wi_ref[0, 0, s * GH:(s + 1) * GH, GP:2 * GP] = (bbt_re * ai + bbt_im * ar).astype(wi_ref.dtype)
    adv_ref[0, 0, 0:1, :] = pw[L][0]
    adv_ref[0, 0, 1:2, :] = pw[L][1]


def _s5_prep(lam_re, lam_im, log_step, b_re, b_im, c_re, c_im):
    nl, G, P = lam_re.shape
    H = b_re.shape[-1]
    L, S = SSM_CHUNK, SLAB_GROUPS
    J = G // S
    GH, GP = S * H, S * P
    per_h = lambda a: jnp.repeat(a[:, :, None, :], H, axis=2).reshape(nl, J, GH, a.shape[-1])
    rows = jnp.stack([lam_re.reshape(nl, J, GP), lam_im.reshape(nl, J, GP),
                      jnp.repeat(log_step, P, axis=-1).reshape(nl, J, GP)], axis=2)
    ins = (rows, per_h(lam_re), per_h(lam_im), per_h(log_step[..., None]),
           b_re.transpose(0, 1, 3, 2).reshape(nl, J, GH, P), b_im.transpose(0, 1, 3, 2).reshape(nl, J, GH, P),
           c_re.reshape(nl, J, GH, P), c_im.reshape(nl, J, GH, P))
    outs = ((nl, J, L * GH, 2 * GH), (nl, J, L * GH, 2 * GP), (nl, J, 2 * GP, L * GH), (nl, J, 2, GP))
    dts = (BF16, BF16, BF16, F32)
    blk = lambda shape: pl.BlockSpec((1, 1) + shape[2:], lambda i, j: (i, j, 0, 0))
    return pl.pallas_call(
        _s5_prep_kernel,
        grid=(nl, J),
        in_specs=[blk(a.shape) for a in ins],
        out_specs=[blk(o) for o in outs],
        out_shape=[jax.ShapeDtypeStruct(o, dt) for o, dt in zip(outs, dts)],
        compiler_params=_cparams("parallel", "parallel"),
        name="s5_prep",
    )(*ins)


def _inproj_kernel(h_ref, gain_ref, w_ref, wal_ref, walpha_ref, balpha_ref, u_ref, *rest, widths):
    out_refs, u_s = rest[:-1], rest[-1]
    la_ref = out_refs[-1]
    L = SSM_CHUNK
    zb = _rms(h_ref[...], gain_ref[...]).astype(BF16)
    a_low = _dot(zb, wal_ref[...])
    logits = _dot(a_low.astype(BF16), walpha_ref[...]) + balpha_ref[...]
    log_sig = jnp.minimum(logits, 0.0) - jnp.log(1.0 + jnp.exp(-jnp.abs(logits)))
    la_ref[...] = log_sig * (1.0 / GLA_GATE_TAU)
    nj = u_ref.shape[0]
    u = _dot(zb, w_ref[:, :nj * LANES])
    for j in range(nj):
        u_s[j] = u[:, j * LANES:(j + 1) * LANES]
    rows = u_ref.shape[1]
    for j in range(nj):
        for t in range(L):
            u_ref[j, :, t * LANES:(t + 1) * LANES] = u_s[j, pl.ds(t, rows, stride=L), :].astype(u_ref.dtype)
    start = nj * LANES
    for ref, width in zip(out_refs[:-1], widths):
        ref[...] = _dot(zb, w_ref[:, start:start + width]).astype(ref.dtype)
        start += width


def _inproj(h, gain, w_main, w_al, w_alpha, b_alpha, sw, widths, tm):
    n, d = h.shape
    kd = w_alpha.shape[1]
    L = SSM_CHUNK
    row = lambda w: pl.BlockSpec((tm, w), lambda i: (i, 0))
    nj = sw // LANES
    return pl.pallas_call(
        functools.partial(_inproj_kernel, widths=widths),
        grid=(n // tm,),
        in_specs=[row(d), _const_spec(gain.shape), _const_spec(w_main.shape), _const_spec(w_al.shape),
                  _const_spec(w_alpha.shape), _const_spec(b_alpha.shape)],
        out_specs=[pl.BlockSpec((nj, tm // L, L * LANES), lambda i: (0, i, 0))]
                  + [row(w) for w in widths] + [row(kd)],
        out_shape=[jax.ShapeDtypeStruct((nj, n // L, L * LANES), BF16)]
                  + [jax.ShapeDtypeStruct((n, w), BF16) for w in widths]
                  + [jax.ShapeDtypeStruct((n, kd), F32)],
        scratch_shapes=[pltpu.VMEM((nj, tm, LANES), F32)],
        compiler_params=_cparams("parallel"),
        name="inproj",
    )(h, gain, w_main, w_al, w_alpha, b_alpha)


def _s5_state_kernel(u_ref, win_ref, adv_ref, x_ref, v_s, *, n_chunks, bsz):
    v = _dot(u_ref[0], win_ref[0])
    nv = v_s.shape[0]
    nc = nv // 2
    for p in range(nv):
        v_s[p] = v[:, p * LANES:(p + 1) * LANES]
    a_re = [jnp.broadcast_to(adv_ref[0, 0:1, p * LANES:(p + 1) * LANES], (bsz, LANES)) for p in range(nc)]
    a_im = [jnp.broadcast_to(adv_ref[0, 1:2, p * LANES:(p + 1) * LANES], (bsz, LANES)) for p in range(nc)]

    def step(c, carry):
        rows = pl.ds(c, bsz, stride=n_chunks)
        new = []
        for p in range(nc):
            xr, xi = carry[p], carry[nc + p]
            vr = v_s[p, rows, :]
            vi = v_s[nc + p, rows, :]
            v_s[p, rows, :] = xr
            v_s[nc + p, rows, :] = xi
            new.append((a_re[p] * xr - a_im[p] * xi + vr, a_re[p] * xi + a_im[p] * xr + vi))
        return tuple(n[0] for n in new) + tuple(n[1] for n in new)

    zero = jnp.zeros((bsz, LANES), F32)
    lax.fori_loop(0, n_chunks, step, (zero,) * nv)
    for p in range(nv):
        x_ref[0, :, p * LANES:(p + 1) * LANES] = v_s[p].astype(x_ref.dtype)


def _s5_state(u_rows, win, adv, n_chunks, bsz):
    nj, rows, _ = u_rows.shape
    sw2 = win.shape[-1]
    return pl.pallas_call(
        functools.partial(_s5_state_kernel, n_chunks=n_chunks, bsz=bsz),
        grid=(nj,),
        in_specs=[_lead_spec(u_rows), _lead_spec(win), _lead_spec(adv)],
        out_specs=pl.BlockSpec((1, rows, sw2), lambda i: (i, 0, 0)),
        out_shape=jax.ShapeDtypeStruct((nj, rows, sw2), BF16),
        scratch_shapes=[pltpu.VMEM((sw2 // LANES, rows, LANES), F32)],
        compiler_params=_cparams("parallel"),
        name="s5_state",
    )(u_rows, win, adv)


def _s5_out_kernel(u_ref, x_ref, ts_ref, wo_ref, d_ref, y_ref):
    u = u_ref[0]
    x = x_ref[0]
    nt = u.shape[1] // MXU_TILE
    for b in range(nt):
        cols = slice(b * MXU_TILE, (b + 1) * MXU_TILE)
        y = _dot(u[:, :(b + 1) * MXU_TILE], ts_ref[0, (nt - 1 - b) * MXU_TILE:, :])
        y = y + _dot(x, wo_ref[0, :, cols])
        y = y + u[:, cols].astype(F32) * d_ref[0, :, cols]
        y_ref[0, :, cols] = y.astype(y_ref.dtype)


def _s5_out(u_rows, x_rows, tstack, wout, dflat, tr):
    nj, rows, width = u_rows.shape
    sw2 = x_rows.shape[-1]
    lead2 = lambda a: pl.BlockSpec((1,) + a.shape[1:], lambda j, i: (j,) + (0,) * (a.ndim - 1))
    return pl.pallas_call(
        _s5_out_kernel,
        grid=(nj, rows // tr),
        in_specs=[pl.BlockSpec((1, tr, width), lambda j, i: (j, i, 0)),
                  pl.BlockSpec((1, tr, sw2), lambda j, i: (j, i, 0)),
                  lead2(tstack), lead2(wout), lead2(dflat)],
        out_specs=pl.BlockSpec((1, tr, width), lambda j, i: (j, i, 0)),
        out_shape=jax.ShapeDtypeStruct(u_rows.shape, BF16),
        compiler_params=_cparams("parallel", "parallel"),
        name="s5_out",
    )(u_rows, x_rows, tstack, wout, dflat)


def _gla_kernel(q_ref, k_ref, v_ref, la_ref, r_ref, gain_ref, o_ref, s_ref, *, n_pad, tile, hk, hv):
    i = pl.program_id(1)
    C = GLA_CHUNK

    @pl.when(i == 0)
    def _():
        s_ref[...] = jnp.zeros_like(s_ref)

    nh = GLA_HEADS
    kd = nh * hk
    sh_c, sh_k = C.bit_length() - 1, hk.bit_length() - 1
    tri = (_iota2((C, C), 0) >= _iota2((C, C), 1)).astype(BF16)
    causal = (_iota2((nh * C, C), 0) & (C - 1)) >= _iota2((nh * C, C), 1)
    own_head = (_iota2((nh * C, kd), 0) >> sh_c) == (_iota2((nh * C, kd), 1) >> sh_k)
    eye = (_iota2((kd, kd), 0) == _iota2((kd, kd), 1)).astype(BF16)
    sub = _iota2((BF16_SUBLANES, kd), 0)
    scale = hk ** -0.5

    for c in range(tile // C):
        rows = pl.ds(c * C, C)
        t = i * tile + c * C + _iota2((C, 1), 0)
        g = jnp.where(t >= n_pad, la_ref[rows, :], 0.0)
        g_hi, g_lo = _split_bf16(g)
        bcum = _dot(tri, g_hi) + _dot(tri, g_lo)
        b_last = bcum[C - 1:C, :]
        q_dec = (q_ref[rows, :].astype(F32) * scale * jnp.exp(bcum)).astype(BF16)
        kf = k_ref[rows, :].astype(F32)
        k_intra = (kf * jnp.exp(-bcum)).astype(BF16)
        k_state = kf * jnp.exp(b_last - bcum)
        v = v_ref[rows, :]
        s_old = s_ref[...]
        q_heads = jnp.where(own_head, jnp.concatenate([q_dec] * nh, axis=0), jnp.zeros((), BF16))
        scores = jnp.where(causal, _dot_tr(q_heads, k_intra), 0.0).astype(BF16)
        o_inter = _dot(q_heads, s_old.astype(BF16))
        bl_hi = b_last.astype(BF16).astype(F32)
        tail = jnp.where(sub == 0, bl_hi, jnp.where(sub == 1, b_last - bl_hi, 0.0))
        moved = _dot_tr(eye, jnp.concatenate([k_state, tail], axis=0).astype(BF16))
        k_t = moved[:, :C].astype(BF16)
        decay = jnp.exp(moved[:, C:C + 1] + moved[:, C + 1:C + 2])
        kv = []
        for h in range(nh):
            rs = slice(h * C, (h + 1) * C)
            vs = slice(h * hv, (h + 1) * hv)
            o = _dot(scores[rs, :], v[:, vs]) + o_inter[rs, :]
            on = _rms(o, gain_ref[:, vs])
            rg = r_ref[rows, vs].astype(F32)
            o_ref[rows, vs] = (rg * _sigmoid(rg) * on).astype(o_ref.dtype)
            kv.append(_dot(k_t[h * hk:(h + 1) * hk, :], v[:, vs]))
        s_ref[...] = s_old * decay + jnp.concatenate(kv, axis=0)


def _gla(q, k, v, la, r, gain, bsz, t_pad, n_pad, tile):
    n, kd = q.shape
    vd = v.shape[1]
    tiles = t_pad // tile
    hk, hv = kd // GLA_HEADS, vd // GLA_HEADS
    row = lambda w: pl.BlockSpec((tile, w), lambda b, i: (b * tiles + i, 0))
    return pl.pallas_call(
        functools.partial(_gla_kernel, n_pad=n_pad, tile=tile, hk=hk, hv=hv),
        grid=(bsz, tiles),
        in_specs=[row(kd), row(kd), row(vd), row(kd), row(vd),
                  pl.BlockSpec(gain.shape, lambda b, i: (0, 0))],
        out_specs=row(vd),
        out_shape=jax.ShapeDtypeStruct((n, vd), BF16),
        scratch_shapes=[pltpu.VMEM((kd, hv), F32)],
        compiler_params=_cparams("parallel", "arbitrary"),
        name="gla",
    )(q, k, v, la, r, gain)


def _merge_ffn_kernel(h_ref, ys_ref, gl_ref, ga_ref, gb_ref, wglu_ref, bglu_ref, wpa_ref, wpb_ref,
                      wout_ref, g2_ref, w1_ref, w3_ref, w2_ref, gf_ref, o_ref, y_s, *, ff_chunk, final):
    L = SSM_CHUNK
    nj, rows = ys_ref.shape[0], ys_ref.shape[1]
    for j in range(nj):
        for t in range(L):
            y_s[j, pl.ds(t, rows, stride=L), :] = ys_ref[j, :, t * LANES:(t + 1) * LANES].astype(F32)
    ys = jnp.concatenate([y_s[j] for j in range(nj)], axis=1)
    act = _gelu_tanh(ys)
    gate = _dot(act.astype(BF16), wglu_ref[...]) + bglu_ref[...]
    s5o = act * _sigmoid(gate)
    y_a = _dot(s5o.astype(BF16), wpa_ref[...])
    y_b = _dot(gl_ref[...], wpb_ref[...])
    mixed = _sigmoid(ga_ref[...].astype(F32)) * y_a + _sigmoid(gb_ref[...].astype(F32)) * y_b
    h1 = h_ref[...] + _dot(mixed.astype(BF16), wout_ref[...])
    z2 = _rms(h1, g2_ref[...]).astype(BF16)
    acc = h1
    d_ff = w1_ref.shape[1]
    for j in range(0, d_ff, ff_chunk):
        a = _dot(z2, w1_ref[:, j:j + ff_chunk])
        b = _dot(z2, w3_ref[:, j:j + ff_chunk])
        hid = (a * _sigmoid(a) * b).astype(BF16)
        acc = acc + _dot(hid, w2_ref[j:j + ff_chunk, :])
    if final:
        acc = _rms(acc, gf_ref[...])
    o_ref[...] = acc


def _merge_ffn(h, ys, gl, ga, gb, wglu, bglu, wpa, wpb, wout, g2, w1, w3, w2, gf, tm, ff_chunk, final):
    n, d = h.shape
    L = SSM_CHUNK
    nj = ys.shape[0]
    row = lambda w: pl.BlockSpec((tm, w), lambda i: (i, 0))
    consts = (wglu, bglu, wpa, wpb, wout, g2, w1, w3, w2, gf)
    return pl.pallas_call(
        functools.partial(_merge_ffn_kernel, ff_chunk=ff_chunk, final=final),
        grid=(n // tm,),
        in_specs=[row(d), pl.BlockSpec((nj, tm // L, L * LANES), lambda i: (0, i, 0)),
                  row(gl.shape[1]), row(d), row(d)]
                 + [_const_spec(c.shape) for c in consts],
        out_specs=row(d),
        out_shape=jax.ShapeDtypeStruct((n, d), F32),
        scratch_shapes=[pltpu.VMEM((nj, tm, LANES), F32)],
        compiler_params=_cparams("parallel"),
        name="merge_ffn",
    )(h, ys, gl, ga, gb, *consts)


def _largest_divisor(n, cap, mult):
    best = mult
    for t in range(mult, cap + 1, mult):
        if n % t == 0:
            best = t
    return best


def kernel(x, meta, norm1, w_in, lam_re, lam_im, log_step, b_re, b_im, c_re, c_im, d_skip, w_glu, b_glu, w_pa, w_alpha, b_alpha, gla_norm, w_pb, w_out, norm2, w_ff1, w_ff3, w_ff2, norm_f):
    bsz, seq, d = x.shape
    depth = w_in.shape[0]
    G = lam_re.shape[1]
    sw = G * SSM_GROUP
    kd = w_alpha.shape[2]
    vd = w_pb.shape[1]
    d_ff = w_ff1.shape[2]
    L = SSM_CHUNK
    n_pad = (-(N_META + seq)) % GLA_CHUNK
    t_pad = n_pad + N_META + seq
    n = bsz * t_pad
    n_chunks = t_pad // L
    nj = sw // LANES
    assert bsz % 8 == 0 and t_pad % L == 0 and G % SLAB_GROUPS == 0

    tm = _largest_divisor(n, 512, L * BF16_SUBLANES)
    gla_tile = _largest_divisor(t_pad, 1024, GLA_CHUNK)
    ff_chunk = _largest_divisor(d_ff, MXU_TILE, LANES)
    s5_rows = _largest_divisor(bsz * n_chunks, 1040, BF16_SUBLANES)

    widths = (kd, kd, vd, vd, d, d)
    lo = sw + 2 * kd + 2 * vd
    w_main = jnp.concatenate([w_in[:, :, :lo], w_in[:, :, lo + GLA_GATE_RANK:]], axis=2).astype(BF16)
    w_al = jnp.pad(w_in[:, :, lo:lo + GLA_GATE_RANK], ((0, 0), (0, 0), (0, LANES - GLA_GATE_RANK))).astype(BF16)
    w_alpha_p = jnp.pad(w_alpha, ((0, 0), (0, LANES - GLA_GATE_RANK), (0, 0))).astype(BF16)
    bf = lambda a: a.astype(BF16)
    w_glu_b, w_pa_b, w_pb_b, w_out_b = bf(w_glu), bf(w_pa), bf(w_pb), bf(w_out)
    w1_b, w3_b, w2_b = bf(w_ff1), bf(w_ff3), bf(w_ff2)

    tstack, win, wout, adv = _s5_prep(lam_re, lam_im, log_step, b_re, b_im, c_re, c_im)
    dflat = jnp.tile(d_skip.reshape(depth, nj, 1, LANES), (1, 1, 1, L))

    meta_b = jnp.broadcast_to(meta.astype(x.dtype)[None], (bsz, N_META, d))
    h = jnp.concatenate([jnp.zeros((bsz, n_pad, d), x.dtype), meta_b, x], axis=1).reshape(n, d)

    for l in range(depth):
        u_rows, q, k, v, r, ga, gb, la = _inproj(
            h, norm1[l][None], w_main[l], w_al[l], w_alpha_p[l], b_alpha[l][None], sw, widths, tm)
        x_rows = _s5_state(u_rows, win[l], adv[l], n_chunks, bsz)
        y_rows = _s5_out(u_rows, x_rows, tstack[l], wout[l], dflat[l], s5_rows)
        gl = _gla(q, k, v, la, r, gla_norm[l][None], bsz, t_pad, n_pad, gla_tile)
        h = _merge_ffn(h, y_rows, gl, ga, gb, w_glu_b[l], b_glu[l][None], w_pa_b[l], w_pb_b[l], w_out_b[l],
                       norm2[l][None], w1_b[l], w3_b[l], w2_b[l], norm_f[None], tm, ff_chunk,
                       final=(l == depth - 1))
    return h.reshape(bsz, t_pad, d)[:, n_pad + N_META:]
```

```python
import functools
import math

import jax
import jax.numpy as jnp
from jax import lax
from jax.experimental import pallas as pl
from jax.experimental.pallas import tpu as pltpu

F32 = jnp.float32
BF16 = jnp.bfloat16

EPS = 1e-6
N_META = 16
SSM_GROUP = 16
SSM_CHUNK = 16
GLA_HEADS = 4
GLA_CHUNK = 64
GLA_GATE_RANK = 16
GLA_GATE_TAU = 16.0
LANES = 128
MXU_TILE = 256
BF16_SUBLANES = 16
SLAB_GROUPS = LANES // SSM_GROUP
VMEM_LIMIT = 56 * 1024 * 1024


def _cparams(*sem):
    return pltpu.CompilerParams(dimension_semantics=sem, vmem_limit_bytes=VMEM_LIMIT)


def _const_spec(shape):
    nd = len(shape)
    return pl.BlockSpec(shape, lambda *_: (0,) * nd, pipeline_mode=pl.Buffered(1))


def _lead_spec(a):
    return pl.BlockSpec((1,) + a.shape[1:], lambda i, *_: (i,) + (0,) * (a.ndim - 1))


def _rms(x, gain):
    ms = jnp.mean(x * x, axis=-1, keepdims=True)
    return x * lax.rsqrt(ms + EPS) * gain


def _sigmoid(x):
    return 1.0 / (1.0 + jnp.exp(-x))


def _gelu_tanh(x):
    c = math.sqrt(2.0 / math.pi)
    return x * (0.5 * (1.0 + jnp.tanh(c * (x + 0.044715 * (x * x * x)))))


def _dot(a, b):
    return jnp.dot(a, b, preferred_element_type=F32)


def _dot_tl(a, b):
    return lax.dot_general(a, b, (((0,), (0,)), ((), ())), preferred_element_type=F32)


def _dot_tr(a, b):
    return lax.dot_general(a, b, (((1,), (1,)), ((), ())), preferred_element_type=F32)


def _split_bf16(x):
    hi = x.astype(BF16)
    lo = (x - hi.astype(F32)).astype(BF16)
    return hi, lo


def _iota2(shape, axis):
    return lax.broadcasted_iota(jnp.int32, shape, axis)


def _lam_bar(lr, li, step):
    mag = jnp.exp(lr * step)
    return mag * jnp.cos(li * step), mag * jnp.sin(li * step)


def _cmul(ar, ai, br, bi):
    return ar * br - ai * bi, ar * bi + ai * br


def _dot_tr3(a, b):
    (a_hi, a_lo), (b_hi, b_lo) = _split_bf16(a), _split_bf16(b)
    return _dot_tr(a_hi, b_hi) + _dot_tr(a_hi, b_lo) + _dot_tr(a_lo, b_hi)


def _s5_prep_kernel(rows_ref, lr_ref, li_ref, ls_ref, btr_ref, bti_ref, cr_ref, ci_ref,
                    ts_ref, wi_ref, wo_ref, adv_ref):
    L, H = SSM_CHUNK, SSM_GROUP
    GH, P = cr_ref.shape[2], cr_ref.shape[3]
    GP = rows_ref.shape[3]
    sh_h, sh_p = H.bit_length() - 1, P.bit_length() - 1
    step = jnp.exp(ls_ref[0, 0])
    lr = jnp.minimum(lr_ref[0, 0], -1e-4)
    li = li_ref[0, 0]
    ab_re, ab_im = _lam_bar(lr, li, step)
    den = lr * lr + li * li
    nr = ab_re - 1.0
    coef_re = (nr * lr + ab_im * li) / den
    coef_im = (ab_im * lr - nr * li) / den
    bt_re, bt_im = btr_ref[0, 0], bti_ref[0, 0]
    bb_re = coef_re * bt_re - coef_im * bt_im
    bb_im = coef_re * bt_im + coef_im * bt_re
    c_re, c_im = cr_ref[0, 0], ci_ref[0, 0]
    ca_re, ca_im = [c_re], [c_im]
    for _ in range(L):
        nxt = _cmul(ca_re[-1], ca_im[-1], ab_re, ab_im)
        ca_re.append(nxt[0])
        ca_im.append(nxt[1])

    k_all = (_dot_tr3(bb_re, jnp.concatenate(ca_re[:L], axis=0))
             - _dot_tr3(bb_im, jnp.concatenate(ca_im[:L], axis=0)))
    same = (_iota2(k_all.shape, 0) >> sh_h) == ((_iota2(k_all.shape, 1) & (GH - 1)) >> sh_h)
    k_all = jnp.where(same, k_all, 0.0).astype(ts_ref.dtype)
    k_lag = lambda lag: k_all[:, lag * GH:(lag + 1) * GH]
    nt = L // 2
    for d in range(nt):
        r0 = (nt - 1 - d) * 2 * GH
        ts_ref[0, 0, r0:r0 + GH, 0:GH] = k_lag(2 * d)
        ts_ref[0, 0, r0:r0 + GH, GH:2 * GH] = k_lag(2 * d + 1)
        ts_ref[0, 0, r0 + GH:r0 + 2 * GH, 0:GH] = k_lag(2 * d - 1) if d else jnp.zeros((GH, GH), ts_ref.dtype)
        ts_ref[0, 0, r0 + GH:r0 + 2 * GH, GH:2 * GH] = k_lag(2 * d)

    sel_t = ((_iota2((GP, P), 0) & (P - 1)) == _iota2((GP, P), 1)).astype(BF16)
    move_t = lambda x: sum(_dot_tr(sel_t, part) for part in _split_bf16(x))
    w_re = move_t(jnp.concatenate(ca_re[1:], axis=0))
    w_im = move_t(jnp.concatenate(ca_im[1:], axis=0))
    same = (_iota2(w_re.shape, 0) >> sh_p) == ((_iota2(w_re.shape, 1) & (GH - 1)) >> sh_h)
    wo_ref[0, 0, 0:GP, :] = jnp.where(same, w_re, 0.0).astype(wo_ref.dtype)
    wo_ref[0, 0, GP:2 * GP, :] = jnp.where(same, -w_im, 0.0).astype(wo_ref.dtype)

    sel = ((_iota2((P, GP), 1) & (P - 1)) == _iota2((P, GP), 0)).astype(BF16)
    tile_lanes = lambda x: sum(_dot(part, sel) for part in _split_bf16(x))
    same = (_iota2((GH, GP), 0) >> sh_h) == (_iota2((GH, GP), 1) >> sh_p)
    bbt_re = jnp.where(same, tile_lanes(bb_re), 0.0)
    bbt_im = jnp.where(same, tile_lanes(bb_im), 0.0)
    lr_gp = jnp.minimum(rows_ref[0, 0, 0:1, :], -1e-4)
    a_re, a_im = _lam_bar(lr_gp, rows_ref[0, 0, 1:2, :], jnp.exp(rows_ref[0, 0, 2:3, :]))
    pw = [(jnp.ones_like(a_re), jnp.zeros_like(a_re))]
    for _ in range(L):
        pw.append(_cmul(pw[-1][0], pw[-1][1], a_re, a_im))
    for s in range(L):
        ar, ai = pw[L - 1 - s]
        wi_ref[0, 0, s * GH:(s + 1) * GH, 0:GP] = (bbt_re * ar - bbt_im * ai).astype(wi_ref.dtype)
        wi_ref[0, 0, s * GH:(s + 1) * GH, GP:2 * GP] = (bbt_re * ai + bbt_im * ar).astype(wi_ref.dtype)
    adv_ref[0, 0, 0:1, :] = pw[L][0]
    adv_ref[0, 0, 1:2, :] = pw[L][1]


def _s5_prep(lam_re, lam_im, log_step, b_re, b_im, c_re, c_im):
    nl, G, P = lam_re.shape
    H = b_re.shape[-1]
    L, S = SSM_CHUNK, SLAB_GROUPS
    J = G // S
    GH, GP = S * H, S * P
    per_h = lambda a: jnp.repeat(a[:, :, None, :], H, axis=2).reshape(nl, J, GH, a.shape[-1])
    rows = jnp.stack([lam_re.reshape(nl, J, GP), lam_im.reshape(nl, J, GP),
                      jnp.repeat(log_step, P, axis=-1).reshape(nl, J, GP)], axis=2)
    ins = (rows, per_h(lam_re), per_h(lam_im), per_h(log_step[..., None]),
           b_re.transpose(0, 1, 3, 2).reshape(nl, J, GH, P), b_im.transpose(0, 1, 3, 2).reshape(nl, J, GH, P),
           c_re.reshape(nl, J, GH, P), c_im.reshape(nl, J, GH, P))
    outs = ((nl, J, L * GH, 2 * GH), (nl, J, L * GH, 2 * GP), (nl, J, 2 * GP, L * GH), (nl, J, 2, GP))
    dts = (BF16, BF16, BF16, F32)
    blk = lambda shape: pl.BlockSpec((1, 1) + shape[2:], lambda i, j: (i, j, 0, 0))
    return pl.pallas_call(
        _s5_prep_kernel,
        grid=(nl, J),
        in_specs=[blk(a.shape) for a in ins],
        out_specs=[blk(o) for o in outs],
        out_shape=[jax.ShapeDtypeStruct(o, dt) for o, dt in zip(outs, dts)],
        compiler_params=_cparams("parallel", "parallel"),
        name="s5_prep",
    )(*ins)


def _inproj_kernel(h_ref, gain_ref, w_ref, wg_ref, wal_ref, walpha_ref, balpha_ref, u_ref, *rest, widths):
    out_refs, u_s = rest[:-1], rest[-1]
    la_ref = out_refs[-1]
    n_lo = len(widths) - 2
    L = SSM_CHUNK
    zb = _rms(h_ref[...], gain_ref[...]).astype(BF16)
    a_low = _dot(zb, wal_ref[...])
    logits = _dot(a_low.astype(BF16), walpha_ref[...]) + balpha_ref[...]
    log_sig = jnp.minimum(logits, 0.0) - jnp.log(1.0 + jnp.exp(-jnp.abs(logits)))
    la_ref[...] = log_sig * (1.0 / GLA_GATE_TAU)
    nj = u_ref.shape[0]
    u = _dot(zb, w_ref[:, :nj * LANES])
    for j in range(nj):
        u_s[j] = u[:, j * LANES:(j + 1) * LANES]
    rows = u_ref.shape[1]
    for j in range(nj):
        for t in range(L):
            u_ref[j, :, t * LANES:(t + 1) * LANES] = u_s[j, pl.ds(t, rows, stride=L), :].astype(u_ref.dtype)
    start = nj * LANES
    for ref, width in zip(out_refs[:n_lo], widths[:n_lo]):
        ref[...] = _dot(zb, w_ref[:, start:start + width]).astype(ref.dtype)
        start += width
    start = 0
    for ref, width in zip(out_refs[n_lo:-1], widths[n_lo:]):
        ref[...] = _dot(zb, wg_ref[:, start:start + width]).astype(ref.dtype)
        start += width


def _inproj(h, gain, w_main, w_gates, w_al, w_alpha, b_alpha, sw, widths, tm):
    n, d = h.shape
    kd = w_alpha.shape[1]
    L = SSM_CHUNK
    row = lambda w: pl.BlockSpec((tm, w), lambda i: (i, 0))
    nj = sw // LANES
    return pl.pallas_call(
        functools.partial(_inproj_kernel, widths=widths),
        grid=(n // tm,),
        in_specs=[row(d), _const_spec(gain.shape), _const_spec(w_main.shape), _const_spec(w_gates.shape),
                  _const_spec(w_al.shape), _const_spec(w_alpha.shape), _const_spec(b_alpha.shape)],
        out_specs=[pl.BlockSpec((nj, tm // L, L * LANES), lambda i: (0, i, 0))]
                  + [row(w) for w in widths] + [row(kd)],
        out_shape=[jax.ShapeDtypeStruct((nj, n // L, L * LANES), BF16)]
                  + [jax.ShapeDtypeStruct((n, w), BF16) for w in widths]
                  + [jax.ShapeDtypeStruct((n, kd), F32)],
        scratch_shapes=[pltpu.VMEM((nj, tm, LANES), F32)],
        compiler_params=_cparams("parallel"),
        name="inproj",
    )(h, gain, w_main, w_gates, w_al, w_alpha, b_alpha)


def _s5_state_kernel(u_ref, win_ref, adv_ref, x_ref, v_s, *, n_chunks, bsz):
    v = _dot(u_ref[0], win_ref[0])
    nv = v_s.shape[0]
    nc = nv // 2
    for p in range(nv):
        v_s[p] = v[:, p * LANES:(p + 1) * LANES]
    a_re = [jnp.broadcast_to(adv_ref[0, 0:1, p * LANES:(p + 1) * LANES], (bsz, LANES)) for p in range(nc)]
    a_im = [jnp.broadcast_to(adv_ref[0, 1:2, p * LANES:(p + 1) * LANES], (bsz, LANES)) for p in range(nc)]

    def step(c, carry):
        rows = pl.ds(c, bsz, stride=n_chunks)
        new = []
        for p in range(nc):
            xr, xi = carry[p], carry[nc + p]
            vr = v_s[p, rows, :]
            vi = v_s[nc + p, rows, :]
            v_s[p, rows, :] = xr
            v_s[nc + p, rows, :] = xi
            new.append((a_re[p] * xr - a_im[p] * xi + vr, a_re[p] * xi + a_im[p] * xr + vi))
        return tuple(n[0] for n in new) + tuple(n[1] for n in new)

    zero = jnp.zeros((bsz, LANES), F32)
    lax.fori_loop(0, n_chunks, step, (zero,) * nv)
    for p in range(nv):
        x_ref[0, :, p * LANES:(p + 1) * LANES] = v_s[p].astype(x_ref.dtype)


def _s5_state(u_rows, win, adv, n_chunks, bsz):
    nj, rows, _ = u_rows.shape
    sw2 = win.shape[-1]
    return pl.pallas_call(
        functools.partial(_s5_state_kernel, n_chunks=n_chunks, bsz=bsz),
        grid=(nj,),
        in_specs=[_lead_spec(u_rows), _lead_spec(win), _lead_spec(adv)],
        out_specs=pl.BlockSpec((1, rows, sw2), lambda i: (i, 0, 0)),
        out_shape=jax.ShapeDtypeStruct((nj, rows, sw2), BF16),
        scratch_shapes=[pltpu.VMEM((sw2 // LANES, rows, LANES), F32)],
        compiler_params=_cparams("parallel"),
        name="s5_state",
    )(u_rows, win, adv)


def _s5_out_kernel(u_ref, x_ref, ts_ref, wo_ref, d_ref, y_ref):
    u = u_ref[0]
    x = x_ref[0]
    nt = u.shape[1] // MXU_TILE
    for b in range(nt):
        cols = slice(b * MXU_TILE, (b + 1) * MXU_TILE)
        y = _dot(u[:, :(b + 1) * MXU_TILE], ts_ref[0, (nt - 1 - b) * MXU_TILE:, :])
        y = y + _dot(x, wo_ref[0, :, cols])
        y = y + u[:, cols].astype(F32) * d_ref[0, :, cols]
        y_ref[0, :, cols] = y.astype(y_ref.dtype)


def _s5_out(u_rows, x_rows, tstack, wout, dflat, tr):
    nj, rows, width = u_rows.shape
    sw2 = x_rows.shape[-1]
    lead2 = lambda a: pl.BlockSpec((1,) + a.shape[1:], lambda j, i: (j,) + (0,) * (a.ndim - 1))
    return pl.pallas_call(
        _s5_out_kernel,
        grid=(nj, rows // tr),
        in_specs=[pl.BlockSpec((1, tr, width), lambda j, i: (j, i, 0)),
                  pl.BlockSpec((1, tr, sw2), lambda j, i: (j, i, 0)),
                  lead2(tstack), lead2(wout), lead2(dflat)],
        out_specs=pl.BlockSpec((1, tr, width), lambda j, i: (j, i, 0)),
        out_shape=jax.ShapeDtypeStruct(u_rows.shape, BF16),
        compiler_params=_cparams("parallel", "parallel"),
        name="s5_out",
    )(u_rows, x_rows, tstack, wout, dflat)


def _gla_kernel(q_ref, k_ref, v_ref, la_ref, r_ref, gain_ref, o_ref, s_ref, *, n_pad, tile, hk, hv):
    i = pl.program_id(1)
    C = GLA_CHUNK

    @pl.when(i == 0)
    def _():
        s_ref[...] = jnp.zeros_like(s_ref)

    nh = GLA_HEADS
    kd = nh * hk
    sh_c, sh_k = C.bit_length() - 1, hk.bit_length() - 1
    tri = (_iota2((C, C), 0) >= _iota2((C, C), 1)).astype(BF16)
    causal = (_iota2((nh * C, C), 0) & (C - 1)) >= _iota2((nh * C, C), 1)
    own_head = (_iota2((nh * C, kd), 0) >> sh_c) == (_iota2((nh * C, kd), 1) >> sh_k)
    eye = (_iota2((kd, kd), 0) == _iota2((kd, kd), 1)).astype(BF16)
    sub = _iota2((BF16_SUBLANES, kd), 0)
    scale = hk ** -0.5

    for c in range(tile // C):
        rows = pl.ds(c * C, C)
        t = i * tile + c * C + _iota2((C, 1), 0)
        g = jnp.where(t >= n_pad, la_ref[rows, :], 0.0)
        g_hi, g_lo = _split_bf16(g)
        bcum = _dot(tri, g_hi) + _dot(tri, g_lo)
        b_last = bcum[C - 1:C, :]
        q_dec = (q_ref[rows, :].astype(F32) * scale * jnp.exp(bcum)).astype(BF16)
        kf = k_ref[rows, :].astype(F32)
        k_intra = (kf * jnp.exp(-bcum)).astype(BF16)
        k_state = kf * jnp.exp(b_last - bcum)
        v = v_ref[rows, :]
        s_old = s_ref[...]
        q_heads = jnp.where(own_head, jnp.concatenate([q_dec] * nh, axis=0), jnp.zeros((), BF16))
        scores = jnp.where(causal, _dot_tr(q_heads, k_intra), 0.0).astype(BF16)
        o_inter = _dot(q_heads, s_old.astype(BF16))
        bl_hi = b_last.astype(BF16).astype(F32)
        tail = jnp.where(sub == 0, bl_hi, jnp.where(sub == 1, b_last - bl_hi, 0.0))
        moved = _dot_tr(eye, jnp.concatenate([k_state, tail], axis=0).astype(BF16))
        k_t = moved[:, :C].astype(BF16)
        decay = jnp.exp(moved[:, C:C + 1] + moved[:, C + 1:C + 2])
        kv = []
        for h in range(nh):
            rs = slice(h * C, (h + 1) * C)
            vs = slice(h * hv, (h + 1) * hv)
            o = _dot(scores[rs, :], v[:, vs]) + o_inter[rs, :]
            on = _rms(o, gain_ref[:, vs])
            rg = r_ref[rows, vs].astype(F32)
            o_ref[rows, vs] = (rg * _sigmoid(rg) * on).astype(o_ref.dtype)
            kv.append(_dot(k_t[h * hk:(h + 1) * hk, :], v[:, vs]))
        s_ref[...] = s_old * decay + jnp.concatenate(kv, axis=0)


def _gla(q, k, v, la, r, gain, bsz, t_pad, n_pad, tile):
    n, kd = q.shape
    vd = v.shape[1]
    tiles = t_pad // tile
    hk, hv = kd // GLA_HEADS, vd // GLA_HEADS
    row = lambda w: pl.BlockSpec((tile, w), lambda b, i: (b * tiles + i, 0))
    return pl.pallas_call(
        functools.partial(_gla_kernel, n_pad=n_pad, tile=tile, hk=hk, hv=hv),
        grid=(bsz, tiles),
        in_specs=[row(kd), row(kd), row(vd), row(kd), row(vd),
                  pl.BlockSpec(gain.shape, lambda b, i: (0, 0))],
        out_specs=row(vd),
        out_shape=jax.ShapeDtypeStruct((n, vd), BF16),
        scratch_shapes=[pltpu.VMEM((kd, hv), F32)],
        compiler_params=_cparams("parallel", "arbitrary"),
        name="gla",
    )(q, k, v, la, r, gain)


def _merge_ffn_kernel(h_ref, ys_ref, gl_ref, ga_ref, gb_ref, wglu_ref, bglu_ref, wpa_ref, wpb_ref,
                      wout_ref, g2_ref, w1_ref, w3_ref, w2_ref, gf_ref, o_ref, y_s, *ring,
                      ff_chunk, header, blocks_per_seq):
    L = SSM_CHUNK
    final = bool(ring)
    if final:
        obuf, sem = ring
        i, n_tiles = pl.program_id(0), pl.num_programs(0)
        nb = h_ref.shape[0] // header

        def for_real_blocks(tile, slot, action):
            for k in range(nb):
                gb = tile * nb + k
                seq_id = gb // blocks_per_seq
                blk = gb - seq_id * blocks_per_seq

                @pl.when(blk >= 1)
                def _(k=k, seq_id=seq_id, blk=blk):
                    dst = (seq_id * (blocks_per_seq - 1) + blk - 1) * header
                    action(pltpu.make_async_copy(obuf.at[slot, pl.ds(k * header, header), :],
                                                 o_ref.at[pl.ds(dst, header), :], sem.at[slot]))

        @pl.when(i >= 2)
        def _():
            for_real_blocks(i - 2, i % 2, lambda c: c.wait())

    nj, rows = ys_ref.shape[0], ys_ref.shape[1]
    for j in range(nj):
        for t in range(L):
            y_s[j, pl.ds(t, rows, stride=L), :] = ys_ref[j, :, t * LANES:(t + 1) * LANES].astype(F32)
    ys = jnp.concatenate([y_s[j] for j in range(nj)], axis=1)
    act = _gelu_tanh(ys)
    gate = _dot(act.astype(BF16), wglu_ref[...]) + bglu_ref[...]
    s5o = act * _sigmoid(gate)
    y_a = _dot(s5o.astype(BF16), wpa_ref[...])
    y_b = _dot(gl_ref[...], wpb_ref[...])
    mixed = _sigmoid(ga_ref[...].astype(F32)) * y_a + _sigmoid(gb_ref[...].astype(F32)) * y_b
    h1 = h_ref[...] + _dot(mixed.astype(BF16), wout_ref[...])
    z2 = _rms(h1, g2_ref[...]).astype(BF16)
    acc = h1
    d_ff = w1_ref.shape[1]
    for j in range(0, d_ff, ff_chunk):
        a = _dot(z2, w1_ref[:, j:j + ff_chunk])
        b = _dot(z2, w3_ref[:, j:j + ff_chunk])
        hid = (a * _sigmoid(a) * b).astype(BF16)
        acc = acc + _dot(hid, w2_ref[j:j + ff_chunk, :])
    if not final:
        o_ref[...] = acc
        return
    obuf[i % 2] = _rms(acc, gf_ref[...])
    for_real_blocks(i, i % 2, lambda c: c.start())

    @pl.when(i == n_tiles - 1)
    def _():
        @pl.when(i >= 1)
        def _():
            for_real_blocks(i - 1, (i + 1) % 2, lambda c: c.wait())
        for_real_blocks(i, i % 2, lambda c: c.wait())


def _merge_ffn(h, ys, gl, ga, gb, wglu, bglu, wpa, wpb, wout, g2, w1, w3, w2, gf, tm, ff_chunk,
               final_rows=None, header=None, blocks_per_seq=None):
    n, d = h.shape
    L = SSM_CHUNK
    nj = ys.shape[0]
    final = final_rows is not None
    row = lambda w: pl.BlockSpec((tm, w), lambda i: (i, 0))
    consts = (wglu, bglu, wpa, wpb, wout, g2, w1, w3, w2, gf)
    scratch = [pltpu.VMEM((nj, tm, LANES), F32)]
    if final:
        assert tm % header == 0
        scratch += [pltpu.VMEM((2, tm, d), F32), pltpu.SemaphoreType.DMA((2,))]
    return pl.pallas_call(
        functools.partial(_merge_ffn_kernel, ff_chunk=ff_chunk, header=header, blocks_per_seq=blocks_per_seq),
        grid=(n // tm,),
        in_specs=[row(d), pl.BlockSpec((nj, tm // L, L * LANES), lambda i: (0, i, 0)),
                  row(gl.shape[1]), row(d), row(d)]
                 + [_const_spec(c.shape) for c in consts],
        out_specs=pl.BlockSpec(memory_space=pl.ANY) if final else row(d),
        out_shape=jax.ShapeDtypeStruct((final_rows if final else n, d), F32),
        scratch_shapes=scratch,
        compiler_params=_cparams("arbitrary" if final else "parallel"),
        name="merge_ffn",
    )(h, ys, gl, ga, gb, *consts)


def _largest_divisor(n, cap, mult):
    best = mult
    for t in range(mult, cap + 1, mult):
        if n % t == 0:
            best = t
    return best


def kernel(x, meta, norm1, w_in, lam_re, lam_im, log_step, b_re, b_im, c_re, c_im, d_skip, w_glu, b_glu, w_pa, w_alpha, b_alpha, gla_norm, w_pb, w_out, norm2, w_ff1, w_ff3, w_ff2, norm_f):
    bsz, seq, d = x.shape
    depth = w_in.shape[0]
    G = lam_re.shape[1]
    sw = G * SSM_GROUP
    kd = w_alpha.shape[2]
    vd = w_pb.shape[1]
    d_ff = w_ff1.shape[2]
    L = SSM_CHUNK
    n_pad = (-(N_META + seq)) % GLA_CHUNK
    t_pad = n_pad + N_META + seq
    n = bsz * t_pad
    n_chunks = t_pad // L
    nj = sw // LANES
    header = n_pad + N_META
    assert bsz % 8 == 0 and t_pad % L == 0 and G % SLAB_GROUPS == 0 and seq % header == 0

    tm = _largest_divisor(n, 512, L * BF16_SUBLANES)
    gla_tile = _largest_divisor(t_pad, 1024, GLA_CHUNK)
    ff_chunk = _largest_divisor(d_ff, MXU_TILE, LANES)
    s5_rows = _largest_divisor(bsz * n_chunks, 1040, BF16_SUBLANES)

    widths = (kd, kd, vd, vd, d, d)
    lo = sw + 2 * kd + 2 * vd
    w_main = w_in[:, :, :lo].astype(BF16)
    w_gates = w_in[:, :, lo + GLA_GATE_RANK:].astype(BF16)
    w_al = jnp.pad(w_in[:, :, lo:lo + GLA_GATE_RANK], ((0, 0), (0, 0), (0, LANES - GLA_GATE_RANK))).astype(BF16)
    w_alpha_p = jnp.pad(w_alpha, ((0, 0), (0, LANES - GLA_GATE_RANK), (0, 0))).astype(BF16)
    bf = lambda a: a.astype(BF16)
    w_glu_b, w_pa_b, w_pb_b, w_out_b = bf(w_glu), bf(w_pa), bf(w_pb), bf(w_out)
    w1_b, w3_b, w2_b = bf(w_ff1), bf(w_ff3), bf(w_ff2)

    tstack, win, wout, adv = _s5_prep(lam_re, lam_im, log_step, b_re, b_im, c_re, c_im)
    dflat = jnp.tile(d_skip.reshape(depth, nj, 1, LANES), (1, 1, 1, L))

    meta_b = jnp.broadcast_to(meta.astype(x.dtype)[None], (bsz, N_META, d))
    h = jnp.concatenate([jnp.zeros((bsz, n_pad, d), x.dtype), meta_b, x], axis=1).reshape(n, d)

    for l in range(depth):
        u_rows, q, k, v, r, ga, gb, la = _inproj(
            h, norm1[l][None], w_main[l], w_gates[l], w_al[l], w_alpha_p[l], b_alpha[l][None], sw, widths, tm)
        x_rows = _s5_state(u_rows, win[l], adv[l], n_chunks, bsz)
        y_rows = _s5_out(u_rows, x_rows, tstack[l], wout[l], dflat[l], s5_rows)
        gl = _gla(q, k, v, la, r, gla_norm[l][None], bsz, t_pad, n_pad, gla_tile)
        last = l == depth - 1
        h = _merge_ffn(h, y_rows, gl, ga, gb, w_glu_b[l], b_glu[l][None], w_pa_b[l], w_pb_b[l], w_out_b[l],
                       norm2[l][None], w1_b[l], w3_b[l], w2_b[l], norm_f[None], tm, ff_chunk,
                       **(dict(final_rows=bsz * seq, header=header, blocks_per_seq=t_pad // header) if last else {}))
    return h.reshape(bsz, seq, d)
```

```python
import functools
import math

import jax
import jax.numpy as jnp
from jax import lax
from jax.experimental import pallas as pl
from jax.experimental.pallas import tpu as pltpu

F32 = jnp.float32
BF16 = jnp.bfloat16

EPS = 1e-6
N_META = 16
SSM_GROUP = 16
SSM_CHUNK = 16
GLA_HEADS = 4
GLA_CHUNK = 64
GLA_GATE_RANK = 16
GLA_GATE_TAU = 16.0
LANES = 128
MXU_TILE = 256
BF16_SUBLANES = 16
SLAB_GROUPS = LANES // SSM_GROUP
VMEM_LIMIT = 56 * 1024 * 1024


def _cparams(*sem):
    return pltpu.CompilerParams(dimension_semantics=sem, vmem_limit_bytes=VMEM_LIMIT)


def _layer_spec(a, l):
    nd = a.ndim
    return pl.BlockSpec((None,) + a.shape[1:], lambda *_: (l,) + (0,) * (nd - 1), pipeline_mode=pl.Buffered(1))


def _layer_slab_spec(a, l):
    nd = a.ndim
    return pl.BlockSpec((None, 1) + a.shape[2:], lambda j, *_: (l, j) + (0,) * (nd - 2))


def _lead_spec(a):
    return pl.BlockSpec((1,) + a.shape[1:], lambda i, *_: (i,) + (0,) * (a.ndim - 1))


def _rms(x, gain):
    ms = jnp.mean(x * x, axis=-1, keepdims=True)
    return x * lax.rsqrt(ms + EPS) * gain


def _sigmoid(x):
    return 1.0 / (1.0 + jnp.exp(-x))


def _gelu_tanh(x):
    c = math.sqrt(2.0 / math.pi)
    return x * (0.5 * (1.0 + jnp.tanh(c * (x + 0.044715 * (x * x * x)))))


def _dot(a, b):
    return jnp.dot(a, b, preferred_element_type=F32)


def _dot_tl(a, b):
    return lax.dot_general(a, b, (((0,), (0,)), ((), ())), preferred_element_type=F32)


def _dot_tr(a, b):
    return lax.dot_general(a, b, (((1,), (1,)), ((), ())), preferred_element_type=F32)


def _split_bf16(x):
    hi = x.astype(BF16)
    lo = (x - hi.astype(F32)).astype(BF16)
    return hi, lo


def _iota2(shape, axis):
    return lax.broadcasted_iota(jnp.int32, shape, axis)


def _lam_bar(lr, li, step):
    mag = jnp.exp(lr * step)
    return mag * jnp.cos(li * step), mag * jnp.sin(li * step)


def _cmul(ar, ai, br, bi):
    return ar * br - ai * bi, ar * bi + ai * br


def _dot_tr3(a, b):
    (a_hi, a_lo), (b_hi, b_lo) = _split_bf16(a), _split_bf16(b)
    return _dot_tr(a_hi, b_hi) + _dot_tr(a_hi, b_lo) + _dot_tr(a_lo, b_hi)


def _s5_prep_kernel(rows_ref, lr_ref, li_ref, ls_ref, btr_ref, bti_ref, cr_ref, ci_ref,
                    ts_ref, wi_ref, wo_ref, adv_ref):
    L, H = SSM_CHUNK, SSM_GROUP
    GH, P = cr_ref.shape[2], cr_ref.shape[3]
    GP = rows_ref.shape[3]
    sh_h, sh_p = H.bit_length() - 1, P.bit_length() - 1
    step = jnp.exp(ls_ref[0, 0])
    lr = jnp.minimum(lr_ref[0, 0], -1e-4)
    li = li_ref[0, 0]
    ab_re, ab_im = _lam_bar(lr, li, step)
    den = lr * lr + li * li
    nr = ab_re - 1.0
    coef_re = (nr * lr + ab_im * li) / den
    coef_im = (ab_im * lr - nr * li) / den
    bt_re, bt_im = btr_ref[0, 0], bti_ref[0, 0]
    bb_re = coef_re * bt_re - coef_im * bt_im
    bb_im = coef_re * bt_im + coef_im * bt_re
    c_re, c_im = cr_ref[0, 0], ci_ref[0, 0]
    ca_re, ca_im = [c_re], [c_im]
    for _ in range(L):
        nxt = _cmul(ca_re[-1], ca_im[-1], ab_re, ab_im)
        ca_re.append(nxt[0])
        ca_im.append(nxt[1])

    k_all = (_dot_tr3(bb_re, jnp.concatenate(ca_re[:L], axis=0))
             - _dot_tr3(bb_im, jnp.concatenate(ca_im[:L], axis=0)))
    same = (_iota2(k_all.shape, 0) >> sh_h) == ((_iota2(k_all.shape, 1) & (GH - 1)) >> sh_h)
    k_all = jnp.where(same, k_all, 0.0).astype(ts_ref.dtype)
    k_lag = lambda lag: k_all[:, lag * GH:(lag + 1) * GH]
    nt = L // 2
    for d in range(nt):
        r0 = (nt - 1 - d) * 2 * GH
        ts_ref[0, 0, r0:r0 + GH, 0:GH] = k_lag(2 * d)
        ts_ref[0, 0, r0:r0 + GH, GH:2 * GH] = k_lag(2 * d + 1)
        ts_ref[0, 0, r0 + GH:r0 + 2 * GH, 0:GH] = k_lag(2 * d - 1) if d else jnp.zeros((GH, GH), ts_ref.dtype)
        ts_ref[0, 0, r0 + GH:r0 + 2 * GH, GH:2 * GH] = k_lag(2 * d)

    sel_t = ((_iota2((GP, P), 0) & (P - 1)) == _iota2((GP, P), 1)).astype(BF16)
    move_t = lambda x: sum(_dot_tr(sel_t, part) for part in _split_bf16(x))
    w_re = move_t(jnp.concatenate(ca_re[1:], axis=0))
    w_im = move_t(jnp.concatenate(ca_im[1:], axis=0))
    same = (_iota2(w_re.shape, 0) >> sh_p) == ((_iota2(w_re.shape, 1) & (GH - 1)) >> sh_h)
    wo_ref[0, 0, 0:GP, :] = jnp.where(same, w_re, 0.0).astype(wo_ref.dtype)
    wo_ref[0, 0, GP:2 * GP, :] = jnp.where(same, -w_im, 0.0).astype(wo_ref.dtype)

    sel = ((_iota2((P, GP), 1) & (P - 1)) == _iota2((P, GP), 0)).astype(BF16)
    tile_lanes = lambda x: sum(_dot(part, sel) for part in _split_bf16(x))
    same = (_iota2((GH, GP), 0) >> sh_h) == (_iota2((GH, GP), 1) >> sh_p)
    bbt_re = jnp.where(same, tile_lanes(bb_re), 0.0)
    bbt_im = jnp.where(same, tile_lanes(bb_im), 0.0)
    lr_gp = jnp.minimum(rows_ref[0, 0, 0:1, :], -1e-4)
    a_re, a_im = _lam_bar(lr_gp, rows_ref[0, 0, 1:2, :], jnp.exp(rows_ref[0, 0, 2:3, :]))
    pw = [(jnp.ones_like(a_re), jnp.zeros_like(a_re))]
    for _ in range(L):
        pw.append(_cmul(pw[-1][0], pw[-1][1], a_re, a_im))
    for s in range(L):
        ar, ai = pw[L - 1 - s]
        wi_ref[0, 0, s * GH:(s + 1) * GH, 0:GP] = (bbt_re * ar - bbt_im * ai).astype(wi_ref.dtype)
        wi_ref[0, 0, s * GH:(s + 1) * GH, GP:2 * GP] = (bbt_re * ai + bbt_im * ar).astype(wi_ref.dtype)
    adv_ref[0, 0, 0:1, :] = pw[L][0]
    adv_ref[0, 0, 1:2, :] = pw[L][1]


def _s5_prep(lam_re, lam_im, log_step, b_re, b_im, c_re, c_im):
    nl, G, P = lam_re.shape
    H = b_re.shape[-1]
    L, S = SSM_CHUNK, SLAB_GROUPS
    J = G // S
    GH, GP = S * H, S * P
    per_h = lambda a: jnp.repeat(a[:, :, None, :], H, axis=2).reshape(nl, J, GH, a.shape[-1])
    rows = jnp.stack([lam_re.reshape(nl, J, GP), lam_im.reshape(nl, J, GP),
                      jnp.repeat(log_step, P, axis=-1).reshape(nl, J, GP)], axis=2)
    ins = (rows, per_h(lam_re), per_h(lam_im), per_h(log_step[..., None]),
           b_re.transpose(0, 1, 3, 2).reshape(nl, J, GH, P), b_im.transpose(0, 1, 3, 2).reshape(nl, J, GH, P),
           c_re.reshape(nl, J, GH, P), c_im.reshape(nl, J, GH, P))
    outs = ((nl, J, L * GH, 2 * GH), (nl, J, L * GH, 2 * GP), (nl, J, 2 * GP, L * GH), (nl, J, 2, GP))
    dts = (BF16, BF16, BF16, F32)
    blk = lambda shape: pl.BlockSpec((1, 1) + shape[2:], lambda i, j: (i, j, 0, 0))
    return pl.pallas_call(
        _s5_prep_kernel,
        grid=(nl, J),
        in_specs=[blk(a.shape) for a in ins],
        out_specs=[blk(o) for o in outs],
        out_shape=[jax.ShapeDtypeStruct(o, dt) for o, dt in zip(outs, dts)],
        compiler_params=_cparams("parallel", "parallel"),
        name="s5_prep",
    )(*ins)


def _w_in_split_kernel(w_ref, main_ref, gates_ref, al_ref, *, lo, rank):
    w = w_ref[0]
    main_ref[0] = w[:, :lo].astype(main_ref.dtype)
    gates_ref[0] = w[:, lo + rank:].astype(gates_ref.dtype)
    al_ref[0] = jnp.zeros(al_ref.shape[1:], al_ref.dtype)
    al_ref[0, :, :rank] = w[:, lo:lo + rank].astype(al_ref.dtype)


def _w_in_split(w_in, lo, rank, rows):
    nl, d, cols = w_in.shape
    hi = cols - lo - rank
    blk = lambda w: pl.BlockSpec((1, rows, w), lambda l, i: (l, i, 0))
    return pl.pallas_call(
        functools.partial(_w_in_split_kernel, lo=lo, rank=rank),
        grid=(nl, d // rows),
        in_specs=[blk(cols)],
        out_specs=[blk(lo), blk(hi), blk(LANES)],
        out_shape=[jax.ShapeDtypeStruct((nl, d, lo), BF16), jax.ShapeDtypeStruct((nl, d, hi), BF16),
                   jax.ShapeDtypeStruct((nl, d, LANES), BF16)],
        compiler_params=_cparams("parallel", "parallel"),
        name="w_in_split",
    )(w_in)


def _inproj_kernel(h_ref, gain_ref, w_ref, wg_ref, wal_ref, walpha_ref, balpha_ref, u_ref, *rest, widths):
    out_refs, u_s = rest[:-1], rest[-1]
    la_ref = out_refs[-1]
    n_lo = len(widths) - 2
    L = SSM_CHUNK
    zb = _rms(h_ref[...], gain_ref[...]).astype(BF16)
    a_low = _dot(zb, wal_ref[...])
    logits = _dot(a_low.astype(BF16), walpha_ref[...]) + balpha_ref[...]
    log_sig = jnp.minimum(logits, 0.0) - jnp.log(1.0 + jnp.exp(-jnp.abs(logits)))
    la_ref[...] = log_sig * (1.0 / GLA_GATE_TAU)
    nj = u_ref.shape[0]
    u = _dot(zb, w_ref[:, :nj * LANES])
    for j in range(nj):
        u_s[j] = u[:, j * LANES:(j + 1) * LANES]
    rows = u_ref.shape[1]
    for j in range(nj):
        for t in range(L):
            u_ref[j, :, t * LANES:(t + 1) * LANES] = u_s[j, pl.ds(t, rows, stride=L), :].astype(u_ref.dtype)
    start = nj * LANES
    for ref, width in zip(out_refs[:n_lo], widths[:n_lo]):
        ref[...] = _dot(zb, w_ref[:, start:start + width]).astype(ref.dtype)
        start += width
    start = 0
    for ref, width in zip(out_refs[n_lo:-1], widths[n_lo:]):
        ref[...] = _dot(zb, wg_ref[:, start:start + width]).astype(ref.dtype)
        start += width


def _inproj(l, h, gain, w_main, w_gates, w_al, w_alpha, b_alpha, sw, widths, tm):
    n, d = h.shape
    kd = w_alpha.shape[-1]
    L = SSM_CHUNK
    row = lambda w: pl.BlockSpec((tm, w), lambda i: (i, 0))
    nj = sw // LANES
    return pl.pallas_call(
        functools.partial(_inproj_kernel, widths=widths),
        grid=(n // tm,),
        in_specs=[row(d)] + [_layer_spec(a, l) for a in (gain, w_main, w_gates, w_al, w_alpha, b_alpha)],
        out_specs=[pl.BlockSpec((nj, tm // L, L * LANES), lambda i: (0, i, 0))]
                  + [row(w) for w in widths] + [row(kd)],
        out_shape=[jax.ShapeDtypeStruct((nj, n // L, L * LANES), BF16)]
                  + [jax.ShapeDtypeStruct((n, w), BF16) for w in widths]
                  + [jax.ShapeDtypeStruct((n, kd), F32)],
        scratch_shapes=[pltpu.VMEM((nj, tm, LANES), F32)],
        compiler_params=_cparams("parallel"),
        name="inproj",
    )(h, gain, w_main, w_gates, w_al, w_alpha, b_alpha)


def _s5_state_kernel(u_ref, win_ref, adv_ref, x_ref, v_s, *, n_chunks, bsz):
    v = _dot(u_ref[0], win_ref[0])
    nv = v_s.shape[0]
    nc = nv // 2
    for p in range(nv):
        v_s[p] = v[:, p * LANES:(p + 1) * LANES]
    a_re = [jnp.broadcast_to(adv_ref[0, 0:1, p * LANES:(p + 1) * LANES], (bsz, LANES)) for p in range(nc)]
    a_im = [jnp.broadcast_to(adv_ref[0, 1:2, p * LANES:(p + 1) * LANES], (bsz, LANES)) for p in range(nc)]

    def step(c, carry):
        rows = pl.ds(c, bsz, stride=n_chunks)
        new = []
        for p in range(nc):
            xr, xi = carry[p], carry[nc + p]
            vr = v_s[p, rows, :]
            vi = v_s[nc + p, rows, :]
            v_s[p, rows, :] = xr
            v_s[nc + p, rows, :] = xi
            new.append((a_re[p] * xr - a_im[p] * xi + vr, a_re[p] * xi + a_im[p] * xr + vi))
        return tuple(n[0] for n in new) + tuple(n[1] for n in new)

    zero = jnp.zeros((bsz, LANES), F32)
    lax.fori_loop(0, n_chunks, step, (zero,) * nv)
    for p in range(nv):
        x_ref[0, :, p * LANES:(p + 1) * LANES] = v_s[p].astype(x_ref.dtype)


def _s5_state(l, u_rows, win, adv, n_chunks, bsz):
    nj, rows, _ = u_rows.shape
    sw2 = win.shape[-1]
    return pl.pallas_call(
        functools.partial(_s5_state_kernel, n_chunks=n_chunks, bsz=bsz),
        grid=(nj,),
        in_specs=[_lead_spec(u_rows), _layer_slab_spec(win, l), _layer_slab_spec(adv, l)],
        out_specs=pl.BlockSpec((1, rows, sw2), lambda i: (i, 0, 0)),
        out_shape=jax.ShapeDtypeStruct((nj, rows, sw2), BF16),
        scratch_shapes=[pltpu.VMEM((sw2 // LANES, rows, LANES), F32)],
        compiler_params=_cparams("parallel"),
        name="s5_state",
    )(u_rows, win, adv)


def _s5_out_kernel(u_ref, x_ref, ts_ref, wo_ref, d_ref, y_ref):
    u = u_ref[0]
    x = x_ref[0]
    nt = u.shape[1] // MXU_TILE
    for b in range(nt):
        cols = slice(b * MXU_TILE, (b + 1) * MXU_TILE)
        y = _dot(u[:, :(b + 1) * MXU_TILE], ts_ref[0, (nt - 1 - b) * MXU_TILE:, :])
        y = y + _dot(x, wo_ref[0, :, cols])
        y = y + u[:, cols].astype(F32) * d_ref[0, :, cols]
        y_ref[0, :, cols] = y.astype(y_ref.dtype)


def _s5_out(l, u_rows, x_rows, tstack, wout, dflat, tr):
    nj, rows, width = u_rows.shape
    sw2 = x_rows.shape[-1]
    return pl.pallas_call(
        _s5_out_kernel,
        grid=(nj, rows // tr),
        in_specs=[pl.BlockSpec((1, tr, width), lambda j, i: (j, i, 0)),
                  pl.BlockSpec((1, tr, sw2), lambda j, i: (j, i, 0)),
                  _layer_slab_spec(tstack, l), _layer_slab_spec(wout, l), _layer_slab_spec(dflat, l)],
        out_specs=pl.BlockSpec((1, tr, width), lambda j, i: (j, i, 0)),
        out_shape=jax.ShapeDtypeStruct(u_rows.shape, BF16),
        compiler_params=_cparams("parallel", "parallel"),
        name="s5_out",
    )(u_rows, x_rows, tstack, wout, dflat)


def _gla_kernel(q_ref, k_ref, v_ref, la_ref, r_ref, gain_ref, o_ref, s_ref, *, n_pad, tile, hk, hv):
    i = pl.program_id(1)
    C = GLA_CHUNK

    @pl.when(i == 0)
    def _():
        s_ref[...] = jnp.zeros_like(s_ref)

    nh = GLA_HEADS
    kd = nh * hk
    sh_c, sh_k = C.bit_length() - 1, hk.bit_length() - 1
    tri = (_iota2((C, C), 0) >= _iota2((C, C), 1)).astype(BF16)
    causal = (_iota2((nh * C, C), 0) & (C - 1)) >= _iota2((nh * C, C), 1)
    own_head = (_iota2((nh * C, kd), 0) >> sh_c) == (_iota2((nh * C, kd), 1) >> sh_k)
    eye = (_iota2((kd, kd), 0) == _iota2((kd, kd), 1)).astype(BF16)
    sub = _iota2((BF16_SUBLANES, kd), 0)
    scale = hk ** -0.5

    for c in range(tile // C):
        rows = pl.ds(c * C, C)
        t = i * tile + c * C + _iota2((C, 1), 0)
        g = jnp.where(t >= n_pad, la_ref[rows, :], 0.0)
        g_hi, g_lo = _split_bf16(g)
        bcum = _dot(tri, g_hi) + _dot(tri, g_lo)
        b_last = bcum[C - 1:C, :]
        q_dec = (q_ref[rows, :].astype(F32) * scale * jnp.exp(bcum)).astype(BF16)
        kf = k_ref[rows, :].astype(F32)
        k_intra = (kf * jnp.exp(-bcum)).astype(BF16)
        k_state = kf * jnp.exp(b_last - bcum)
        v = v_ref[rows, :]
        s_old = s_ref[...]
        q_heads = jnp.where(own_head, jnp.concatenate([q_dec] * nh, axis=0), jnp.zeros((), BF16))
        scores = jnp.where(causal, _dot_tr(q_heads, k_intra), 0.0).astype(BF16)
        o_inter = _dot(q_heads, s_old.astype(BF16))
        bl_hi = b_last.astype(BF16).astype(F32)
        tail = jnp.where(sub == 0, bl_hi, jnp.where(sub == 1, b_last - bl_hi, 0.0))
        moved = _dot_tr(eye, jnp.concatenate([k_state, tail], axis=0).astype(BF16))
        k_t = moved[:, :C].astype(BF16)
        decay = jnp.exp(moved[:, C:C + 1] + moved[:, C + 1:C + 2])
        kv = []
        for h in range(nh):
            rs = slice(h * C, (h + 1) * C)
            vs = slice(h * hv, (h + 1) * hv)
            o = _dot(scores[rs, :], v[:, vs]) + o_inter[rs, :]
            on = _rms(o, gain_ref[:, vs])
            rg = r_ref[rows, vs].astype(F32)
            o_ref[rows, vs] = (rg * _sigmoid(rg) * on).astype(o_ref.dtype)
            kv.append(_dot(k_t[h * hk:(h + 1) * hk, :], v[:, vs]))
        s_ref[...] = s_old * decay + jnp.concatenate(kv, axis=0)


def _gla(l, q, k, v, la, r, gain, bsz, t_pad, n_pad, tile):
    n, kd = q.shape
    vd = v.shape[1]
    tiles = t_pad // tile
    hk, hv = kd // GLA_HEADS, vd // GLA_HEADS
    row = lambda w: pl.BlockSpec((tile, w), lambda b, i: (b * tiles + i, 0))
    return pl.pallas_call(
        functools.partial(_gla_kernel, n_pad=n_pad, tile=tile, hk=hk, hv=hv),
        grid=(bsz, tiles),
        in_specs=[row(kd), row(kd), row(vd), row(kd), row(vd), _layer_spec(gain, l)],
        out_specs=row(vd),
        out_shape=jax.ShapeDtypeStruct((n, vd), BF16),
        scratch_shapes=[pltpu.VMEM((kd, hv), F32)],
        compiler_params=_cparams("parallel", "arbitrary"),
        name="gla",
    )(q, k, v, la, r, gain)


def _merge_ffn_kernel(h_ref, ys_ref, gl_ref, ga_ref, gb_ref, wglu_ref, bglu_ref, wpa_ref, wpb_ref,
                      wout_ref, g2_ref, w1_ref, w3_ref, w2_ref, gf_ref, o_ref, y_s, *ring,
                      ff_chunk, header, blocks_per_seq):
    L = SSM_CHUNK
    final = bool(ring)
    if final:
        obuf, sem = ring
        i, n_tiles = pl.program_id(0), pl.num_programs(0)
        nb = h_ref.shape[0] // header

        def for_real_blocks(tile, slot, action):
            for k in range(nb):
                gb = tile * nb + k
                seq_id = gb // blocks_per_seq
                blk = gb - seq_id * blocks_per_seq

                @pl.when(blk >= 1)
                def _(k=k, seq_id=seq_id, blk=blk):
                    dst = (seq_id * (blocks_per_seq - 1) + blk - 1) * header
                    action(pltpu.make_async_copy(obuf.at[slot, pl.ds(k * header, header), :],
                                                 o_ref.at[pl.ds(dst, header), :], sem.at[slot]))

        @pl.when(i >= 2)
        def _():
            for_real_blocks(i - 2, i % 2, lambda c: c.wait())

    nj, rows = ys_ref.shape[0], ys_ref.shape[1]
    for j in range(nj):
        for t in range(L):
            y_s[j, pl.ds(t, rows, stride=L), :] = ys_ref[j, :, t * LANES:(t + 1) * LANES].astype(F32)
    ys = jnp.concatenate([y_s[j] for j in range(nj)], axis=1)
    act = _gelu_tanh(ys)
    gate = _dot(act.astype(BF16), wglu_ref[...]) + bglu_ref[...]
    s5o = act * _sigmoid(gate)
    y_a = _dot(s5o.astype(BF16), wpa_ref[...])
    y_b = _dot(gl_ref[...], wpb_ref[...])
    mixed = _sigmoid(ga_ref[...].astype(F32)) * y_a + _sigmoid(gb_ref[...].astype(F32)) * y_b
    h1 = h_ref[...] + _dot(mixed.astype(BF16), wout_ref[...])
    z2 = _rms(h1, g2_ref[...]).astype(BF16)
    acc = h1
    d_ff = w1_ref.shape[1]
    for j in range(0, d_ff, ff_chunk):
        a = _dot(z2, w1_ref[:, j:j + ff_chunk])
        b = _dot(z2, w3_ref[:, j:j + ff_chunk])
        hid = (a * _sigmoid(a) * b).astype(BF16)
        acc = acc + _dot(hid, w2_ref[j:j + ff_chunk, :])
    if not final:
        o_ref[...] = acc
        return
    obuf[i % 2] = _rms(acc, gf_ref[...])
    for_real_blocks(i, i % 2, lambda c: c.start())

    @pl.when(i == n_tiles - 1)
    def _():
        @pl.when(i >= 1)
        def _():
            for_real_blocks(i - 1, (i + 1) % 2, lambda c: c.wait())
        for_real_blocks(i, i % 2, lambda c: c.wait())


def _merge_ffn(l, h, ys, gl, ga, gb, wglu, bglu, wpa, wpb, wout, g2, w1, w3, w2, gf, tm, ff_chunk,
               final_rows=None, header=None, blocks_per_seq=None):
    n, d = h.shape
    L = SSM_CHUNK
    nj = ys.shape[0]
    final = final_rows is not None
    row = lambda w: pl.BlockSpec((tm, w), lambda i: (i, 0))
    consts = (wglu, bglu, wpa, wpb, wout, g2, w1, w3, w2, gf)
    scratch = [pltpu.VMEM((nj, tm, LANES), F32)]
    if final:
        assert tm % header == 0
        scratch += [pltpu.VMEM((2, tm, d), F32), pltpu.SemaphoreType.DMA((2,))]
    return pl.pallas_call(
        functools.partial(_merge_ffn_kernel, ff_chunk=ff_chunk, header=header, blocks_per_seq=blocks_per_seq),
        grid=(n // tm,),
        in_specs=[row(d), pl.BlockSpec((nj, tm // L, L * LANES), lambda i: (0, i, 0)),
                  row(gl.shape[1]), row(d), row(d)]
                 + [_layer_spec(c, l) for c in consts[:-1]] + [_layer_spec(gf, 0)],
        out_specs=pl.BlockSpec(memory_space=pl.ANY) if final else row(d),
        out_shape=jax.ShapeDtypeStruct((final_rows if final else n, d), F32),
        scratch_shapes=scratch,
        compiler_params=_cparams("arbitrary" if final else "parallel"),
        name="merge_ffn",
    )(h, ys, gl, ga, gb, *consts)


def _largest_divisor(n, cap, mult):
    best = mult
    for t in range(mult, cap + 1, mult):
        if n % t == 0:
            best = t
    return best


def kernel(x, meta, norm1, w_in, lam_re, lam_im, log_step, b_re, b_im, c_re, c_im, d_skip, w_glu, b_glu, w_pa, w_alpha, b_alpha, gla_norm, w_pb, w_out, norm2, w_ff1, w_ff3, w_ff2, norm_f):
    bsz, seq, d = x.shape
    depth = w_in.shape[0]
    G = lam_re.shape[1]
    sw = G * SSM_GROUP
    kd = w_alpha.shape[2]
    vd = w_pb.shape[1]
    d_ff = w_ff1.shape[2]
    L = SSM_CHUNK
    n_pad = (-(N_META + seq)) % GLA_CHUNK
    t_pad = n_pad + N_META + seq
    n = bsz * t_pad
    n_chunks = t_pad // L
    nj = sw // LANES
    header = n_pad + N_META
    assert bsz % 8 == 0 and t_pad % L == 0 and G % SLAB_GROUPS == 0 and seq % header == 0

    tm = _largest_divisor(n, 512, L * BF16_SUBLANES)
    gla_tile = _largest_divisor(t_pad, 1024, GLA_CHUNK)
    ff_chunk = _largest_divisor(d_ff, MXU_TILE, LANES)
    s5_rows = _largest_divisor(bsz * n_chunks, 1040, BF16_SUBLANES)

    widths = (kd, kd, vd, vd, d, d)
    lo = sw + 2 * kd + 2 * vd
    w_main, w_gates, w_al = _w_in_split(w_in, lo, GLA_GATE_RANK, _largest_divisor(d, 256, BF16_SUBLANES))
    w_alpha_p = jnp.pad(w_alpha, ((0, 0), (0, LANES - GLA_GATE_RANK), (0, 0))).astype(BF16)
    bf = lambda a: a.astype(BF16)
    w_glu_b, w_pa_b, w_pb_b, w_out_b = bf(w_glu), bf(w_pa), bf(w_pb), bf(w_out)
    w1_b, w3_b, w2_b = bf(w_ff1), bf(w_ff3), bf(w_ff2)

    tstack, win, wout, adv = _s5_prep(lam_re, lam_im, log_step, b_re, b_im, c_re, c_im)
    dflat = jnp.tile(d_skip.reshape(depth, nj, 1, LANES), (1, 1, 1, L))

    meta_b = jnp.broadcast_to(meta.astype(x.dtype)[None], (bsz, N_META, d))
    h = jnp.concatenate([jnp.zeros((bsz, n_pad, d), x.dtype), meta_b, x], axis=1).reshape(n, d)

    vec = lambda a: a[:, None, :]
    for l in range(depth):
        u_rows, q, k, v, r, ga, gb, la = _inproj(
            l, h, vec(norm1), w_main, w_gates, w_al, w_alpha_p, vec(b_alpha), sw, widths, tm)
        x_rows = _s5_state(l, u_rows, win, adv, n_chunks, bsz)
        y_rows = _s5_out(l, u_rows, x_rows, tstack, wout, dflat, s5_rows)
        gl = _gla(l, q, k, v, la, r, vec(gla_norm), bsz, t_pad, n_pad, gla_tile)
        last = l == depth - 1
        h = _merge_ffn(l, h, y_rows, gl, ga, gb, w_glu_b, vec(b_glu), w_pa_b, w_pb_b, w_out_b,
                       vec(norm2), w1_b, w3_b, w2_b, norm_f[None, None, :], tm, ff_chunk,
                       **(dict(final_rows=bsz * seq, header=header, blocks_per_seq=t_pad // header) if last else {}))
    return h.reshape(bsz, seq, d)
```

```python
import functools
import math

import jax
import jax.numpy as jnp
from jax import lax
from jax.experimental import pallas as pl
from jax.experimental.pallas import tpu as pltpu

F32 = jnp.float32
BF16 = jnp.bfloat16

EPS = 1e-6
N_META = 16
SSM_GROUP = 16
SSM_CHUNK = 16
GLA_HEADS = 4
GLA_CHUNK = 64
GLA_GATE_RANK = 16
GLA_GATE_TAU = 16.0
LANES = 128
MXU_TILE = 256
BF16_SUBLANES = 16
SLAB_GROUPS = LANES // SSM_GROUP
VMEM_LIMIT = 56 * 1024 * 1024


def _cparams(*sem):
    return pltpu.CompilerParams(dimension_semantics=sem, vmem_limit_bytes=VMEM_LIMIT)


def _layer_spec(a, l):
    nd = a.ndim
    return pl.BlockSpec((None,) + a.shape[1:], lambda *_: (l,) + (0,) * (nd - 1), pipeline_mode=pl.Buffered(1))


def _layer_slab_spec(a, l):
    nd = a.ndim
    return pl.BlockSpec((None, 1) + a.shape[2:], lambda j, *_: (l, j) + (0,) * (nd - 2))


def _lead_spec(a):
    return pl.BlockSpec((1,) + a.shape[1:], lambda i, *_: (i,) + (0,) * (a.ndim - 1))


def _rms(x, gain):
    ms = jnp.mean(x * x, axis=-1, keepdims=True)
    return x * lax.rsqrt(ms + EPS) * gain


def _sigmoid(x):
    return 1.0 / (1.0 + jnp.exp(-x))


def _gelu_tanh(x):
    c = math.sqrt(2.0 / math.pi)
    return x * (0.5 * (1.0 + jnp.tanh(c * (x + 0.044715 * (x * x * x)))))


def _dot(a, b):
    return jnp.dot(a, b, preferred_element_type=F32)


def _dot_tl(a, b):
    return lax.dot_general(a, b, (((0,), (0,)), ((), ())), preferred_element_type=F32)


def _dot_tr(a, b):
    return lax.dot_general(a, b, (((1,), (1,)), ((), ())), preferred_element_type=F32)


def _split_bf16(x):
    hi = x.astype(BF16)
    lo = (x - hi.astype(F32)).astype(BF16)
    return hi, lo


def _iota2(shape, axis):
    return lax.broadcasted_iota(jnp.int32, shape, axis)


def _lam_bar(lr, li, step):
    mag = jnp.exp(lr * step)
    return mag * jnp.cos(li * step), mag * jnp.sin(li * step)


def _cmul(ar, ai, br, bi):
    return ar * br - ai * bi, ar * bi + ai * br


def _dot_tr3(a, b):
    (a_hi, a_lo), (b_hi, b_lo) = _split_bf16(a), _split_bf16(b)
    return _dot_tr(a_hi, b_hi) + _dot_tr(a_hi, b_lo) + _dot_tr(a_lo, b_hi)


def _s5_prep_kernel(rows_ref, lr_ref, li_ref, ls_ref, btr_ref, bti_ref, cr_ref, ci_ref,
                    ts_ref, wi_ref, wo_ref, adv_ref):
    L, H = SSM_CHUNK, SSM_GROUP
    GH, P = cr_ref.shape[2], cr_ref.shape[3]
    GP = rows_ref.shape[3]
    sh_h, sh_p = H.bit_length() - 1, P.bit_length() - 1
    step = jnp.exp(ls_ref[0, 0])
    lr = jnp.minimum(lr_ref[0, 0], -1e-4)
    li = li_ref[0, 0]
    ab_re, ab_im = _lam_bar(lr, li, step)
    den = lr * lr + li * li
    nr = ab_re - 1.0
    coef_re = (nr * lr + ab_im * li) / den
    coef_im = (ab_im * lr - nr * li) / den
    bt_re, bt_im = btr_ref[0, 0], bti_ref[0, 0]
    bb_re = coef_re * bt_re - coef_im * bt_im
    bb_im = coef_re * bt_im + coef_im * bt_re
    c_re, c_im = cr_ref[0, 0], ci_ref[0, 0]
    ca_re, ca_im = [c_re], [c_im]
    for _ in range(L):
        nxt = _cmul(ca_re[-1], ca_im[-1], ab_re, ab_im)
        ca_re.append(nxt[0])
        ca_im.append(nxt[1])

    k_all = (_dot_tr3(bb_re, jnp.concatenate(ca_re[:L], axis=0))
             - _dot_tr3(bb_im, jnp.concatenate(ca_im[:L], axis=0)))
    same = (_iota2(k_all.shape, 0) >> sh_h) == ((_iota2(k_all.shape, 1) & (GH - 1)) >> sh_h)
    k_all = jnp.where(same, k_all, 0.0).astype(ts_ref.dtype)
    k_lag = lambda lag: k_all[:, lag * GH:(lag + 1) * GH]
    nt = L // 2
    for d in range(nt):
        r0 = (nt - 1 - d) * 2 * GH
        ts_ref[0, 0, r0:r0 + GH, 0:GH] = k_lag(2 * d)
        ts_ref[0, 0, r0:r0 + GH, GH:2 * GH] = k_lag(2 * d + 1)
        ts_ref[0, 0, r0 + GH:r0 + 2 * GH, 0:GH] = k_lag(2 * d - 1) if d else jnp.zeros((GH, GH), ts_ref.dtype)
        ts_ref[0, 0, r0 + GH:r0 + 2 * GH, GH:2 * GH] = k_lag(2 * d)

    sel_t = ((_iota2((GP, P), 0) & (P - 1)) == _iota2((GP, P), 1)).astype(BF16)
    move_t = lambda x: sum(_dot_tr(sel_t, part) for part in _split_bf16(x))
    w_re = move_t(jnp.concatenate(ca_re[1:], axis=0))
    w_im = move_t(jnp.concatenate(ca_im[1:], axis=0))
    same = (_iota2(w_re.shape, 0) >> sh_p) == ((_iota2(w_re.shape, 1) & (GH - 1)) >> sh_h)
    wo_ref[0, 0, 0:GP, :] = jnp.where(same, w_re, 0.0).astype(wo_ref.dtype)
    wo_ref[0, 0, GP:2 * GP, :] = jnp.where(same, -w_im, 0.0).astype(wo_ref.dtype)

    sel = ((_iota2((P, GP), 1) & (P - 1)) == _iota2((P, GP), 0)).astype(BF16)
    tile_lanes = lambda x: sum(_dot(part, sel) for part in _split_bf16(x))
    same = (_iota2((GH, GP), 0) >> sh_h) == (_iota2((GH, GP), 1) >> sh_p)
    bbt_re = jnp.where(same, tile_lanes(bb_re), 0.0)
    bbt_im = jnp.where(same, tile_lanes(bb_im), 0.0)
    lr_gp = jnp.minimum(rows_ref[0, 0, 0:1, :], -1e-4)
    a_re, a_im = _lam_bar(lr_gp, rows_ref[0, 0, 1:2, :], jnp.exp(rows_ref[0, 0, 2:3, :]))
    pw = [(jnp.ones_like(a_re), jnp.zeros_like(a_re))]
    for _ in range(L):
        pw.append(_cmul(pw[-1][0], pw[-1][1], a_re, a_im))
    for s in range(L):
        ar, ai = pw[L - 1 - s]
        wi_ref[0, 0, s * GH:(s + 1) * GH, 0:GP] = (bbt_re * ar - bbt_im * ai).astype(wi_ref.dtype)
        wi_ref[0, 0, s * GH:(s + 1) * GH, GP:2 * GP] = (bbt_re * ai + bbt_im * ar).astype(wi_ref.dtype)
    adv_ref[0, 0, 0:1, :] = pw[L][0]
    adv_ref[0, 0, 1:2, :] = pw[L][1]


def _s5_prep(lam_re, lam_im, log_step, b_re, b_im, c_re, c_im):
    nl, G, P = lam_re.shape
    H = b_re.shape[-1]
    L, S = SSM_CHUNK, SLAB_GROUPS
    J = G // S
    GH, GP = S * H, S * P
    per_h = lambda a: jnp.repeat(a[:, :, None, :], H, axis=2).reshape(nl, J, GH, a.shape[-1])
    rows = jnp.stack([lam_re.reshape(nl, J, GP), lam_im.reshape(nl, J, GP),
                      jnp.repeat(log_step, P, axis=-1).reshape(nl, J, GP)], axis=2)
    ins = (rows, per_h(lam_re), per_h(lam_im), per_h(log_step[..., None]),
           b_re.transpose(0, 1, 3, 2).reshape(nl, J, GH, P), b_im.transpose(0, 1, 3, 2).reshape(nl, J, GH, P),
           c_re.reshape(nl, J, GH, P), c_im.reshape(nl, J, GH, P))
    outs = ((nl, J, L * GH, 2 * GH), (nl, J, L * GH, 2 * GP), (nl, J, 2 * GP, L * GH), (nl, J, 2, GP))
    dts = (BF16, BF16, BF16, F32)
    blk = lambda shape: pl.BlockSpec((1, 1) + shape[2:], lambda i, j: (i, j, 0, 0))
    return pl.pallas_call(
        _s5_prep_kernel,
        grid=(nl, J),
        in_specs=[blk(a.shape) for a in ins],
        out_specs=[blk(o) for o in outs],
        out_shape=[jax.ShapeDtypeStruct(o, dt) for o, dt in zip(outs, dts)],
        compiler_params=_cparams("parallel", "parallel"),
        name="s5_prep",
    )(*ins)


def _w_in_split_kernel(w_ref, main_ref, gates_ref, al_ref, *, lo, rank):
    w = w_ref[0]
    main_ref[0] = w[:, :lo].astype(main_ref.dtype)
    gates_ref[0] = w[:, lo + rank:].astype(gates_ref.dtype)
    al_ref[0] = jnp.zeros(al_ref.shape[1:], al_ref.dtype)
    al_ref[0, :, :rank] = w[:, lo:lo + rank].astype(al_ref.dtype)


def _w_in_split(w_in, lo, rank, rows):
    nl, d, cols = w_in.shape
    hi = cols - lo - rank
    blk = lambda w: pl.BlockSpec((1, rows, w), lambda l, i: (l, i, 0))
    return pl.pallas_call(
        functools.partial(_w_in_split_kernel, lo=lo, rank=rank),
        grid=(nl, d // rows),
        in_specs=[blk(cols)],
        out_specs=[blk(lo), blk(hi), blk(LANES)],
        out_shape=[jax.ShapeDtypeStruct((nl, d, lo), BF16), jax.ShapeDtypeStruct((nl, d, hi), BF16),
                   jax.ShapeDtypeStruct((nl, d, LANES), BF16)],
        compiler_params=_cparams("parallel", "parallel"),
        name="w_in_split",
    )(w_in)


def _inproj_kernel(h_ref, gain_ref, w_ref, wg_ref, wal_ref, walpha_ref, balpha_ref, u_ref, *rest, widths):
    out_refs, u_s = rest[:-1], rest[-1]
    la_ref = out_refs[-1]
    n_lo = len(widths) - 2
    L = SSM_CHUNK
    zb = _rms(h_ref[...], gain_ref[...]).astype(BF16)
    a_low = _dot(zb, wal_ref[...])
    logits = _dot(a_low.astype(BF16), walpha_ref[...]) + balpha_ref[...]
    log_sig = jnp.minimum(logits, 0.0) - jnp.log(1.0 + jnp.exp(-jnp.abs(logits)))
    la_ref[...] = log_sig * (1.0 / GLA_GATE_TAU)
    nj = u_ref.shape[0]
    u = _dot(zb, w_ref[:, :nj * LANES])
    for j in range(nj):
        u_s[j] = u[:, j * LANES:(j + 1) * LANES]
    rows = u_ref.shape[1]
    for j in range(nj):
        for t in range(L):
            u_ref[j, :, t * LANES:(t + 1) * LANES] = u_s[j, pl.ds(t, rows, stride=L), :].astype(u_ref.dtype)
    start = nj * LANES
    for ref, width in zip(out_refs[:n_lo], widths[:n_lo]):
        ref[...] = _dot(zb, w_ref[:, start:start + width]).astype(ref.dtype)
        start += width
    start = 0
    for ref, width in zip(out_refs[n_lo:-1], widths[n_lo:]):
        ref[...] = _dot(zb, wg_ref[:, start:start + width]).astype(ref.dtype)
        start += width


def _inproj(l, h, gain, w_main, w_gates, w_al, w_alpha, b_alpha, sw, widths, tm):
    n, d = h.shape
    kd = w_alpha.shape[-1]
    L = SSM_CHUNK
    row = lambda w: pl.BlockSpec((tm, w), lambda i: (i, 0))
    nj = sw // LANES
    return pl.pallas_call(
        functools.partial(_inproj_kernel, widths=widths),
        grid=(n // tm,),
        in_specs=[row(d)] + [_layer_spec(a, l) for a in (gain, w_main, w_gates, w_al, w_alpha, b_alpha)],
        out_specs=[pl.BlockSpec((nj, tm // L, L * LANES), lambda i: (0, i, 0))]
                  + [row(w) for w in widths] + [row(kd)],
        out_shape=[jax.ShapeDtypeStruct((nj, n // L, L * LANES), BF16)]
                  + [jax.ShapeDtypeStruct((n, w), BF16) for w in widths]
                  + [jax.ShapeDtypeStruct((n, kd), F32)],
        scratch_shapes=[pltpu.VMEM((nj, tm, LANES), F32)],
        compiler_params=_cparams("parallel"),
        name="inproj",
    )(h, gain, w_main, w_gates, w_al, w_alpha, b_alpha)


def _s5_state_kernel(u_ref, win_ref, adv_ref, x_ref, v_s, *, n_chunks, bsz):
    v = _dot(u_ref[0], win_ref[0])
    nv = v_s.shape[0]
    nc = nv // 2
    for p in range(nv):
        v_s[p] = v[:, p * LANES:(p + 1) * LANES]
    a_re = [jnp.broadcast_to(adv_ref[0, 0:1, p * LANES:(p + 1) * LANES], (bsz, LANES)) for p in range(nc)]
    a_im = [jnp.broadcast_to(adv_ref[0, 1:2, p * LANES:(p + 1) * LANES], (bsz, LANES)) for p in range(nc)]

    def step(c, carry):
        rows = pl.ds(c, bsz, stride=n_chunks)
        new = []
        for p in range(nc):
            xr, xi = carry[p], carry[nc + p]
            vr = v_s[p, rows, :]
            vi = v_s[nc + p, rows, :]
            v_s[p, rows, :] = xr
            v_s[nc + p, rows, :] = xi
            new.append((a_re[p] * xr - a_im[p] * xi + vr, a_re[p] * xi + a_im[p] * xr + vi))
        return tuple(n[0] for n in new) + tuple(n[1] for n in new)

    zero = jnp.zeros((bsz, LANES), F32)
    lax.fori_loop(0, n_chunks, step, (zero,) * nv)
    for p in range(nv):
        x_ref[0, :, p * LANES:(p + 1) * LANES] = v_s[p].astype(x_ref.dtype)


def _s5_state(l, u_rows, win, adv, n_chunks, bsz):
    nj, rows, _ = u_rows.shape
    sw2 = win.shape[-1]
    return pl.pallas_call(
        functools.partial(_s5_state_kernel, n_chunks=n_chunks, bsz=bsz),
        grid=(nj,),
        in_specs=[_lead_spec(u_rows), _layer_slab_spec(win, l), _layer_slab_spec(adv, l)],
        out_specs=pl.BlockSpec((1, rows, sw2), lambda i: (i, 0, 0)),
        out_shape=jax.ShapeDtypeStruct((nj, rows, sw2), BF16),
        scratch_shapes=[pltpu.VMEM((sw2 // LANES, rows, LANES), F32)],
        compiler_params=_cparams("parallel"),
        name="s5_state",
    )(u_rows, win, adv)


def _s5_out_kernel(u_ref, x_ref, ts_ref, wo_ref, d_ref, y_ref):
    u = u_ref[0]
    x = x_ref[0]
    nt = u.shape[1] // MXU_TILE
    for b in range(nt):
        cols = slice(b * MXU_TILE, (b + 1) * MXU_TILE)
        y = _dot(u[:, :(b + 1) * MXU_TILE], ts_ref[0, (nt - 1 - b) * MXU_TILE:, :])
        y = y + _dot(x, wo_ref[0, :, cols])
        y = y + u[:, cols].astype(F32) * d_ref[0, :, cols]
        y_ref[0, :, cols] = y.astype(y_ref.dtype)


def _s5_out(l, u_rows, x_rows, tstack, wout, dflat, tr):
    nj, rows, width = u_rows.shape
    sw2 = x_rows.shape[-1]
    return pl.pallas_call(
        _s5_out_kernel,
        grid=(nj, rows // tr),
        in_specs=[pl.BlockSpec((1, tr, width), lambda j, i: (j, i, 0)),
                  pl.BlockSpec((1, tr, sw2), lambda j, i: (j, i, 0)),
                  _layer_slab_spec(tstack, l), _layer_slab_spec(wout, l), _layer_slab_spec(dflat, l)],
        out_specs=pl.BlockSpec((1, tr, width), lambda j, i: (j, i, 0)),
        out_shape=jax.ShapeDtypeStruct(u_rows.shape, BF16),
        compiler_params=_cparams("parallel", "parallel"),
        name="s5_out",
    )(u_rows, x_rows, tstack, wout, dflat)


def _gla_kernel(q_ref, k_ref, v_ref, la_ref, r_ref, gain_ref, o_ref, s_ref, *, n_pad, tile, hk, hv):
    i = pl.program_id(1)
    C = GLA_CHUNK

    @pl.when(i == 0)
    def _():
        s_ref[...] = jnp.zeros_like(s_ref)

    nh = GLA_HEADS
    kd = nh * hk
    sh_c, sh_k = C.bit_length() - 1, hk.bit_length() - 1
    tri = (_iota2((C, C), 0) >= _iota2((C, C), 1)).astype(BF16)
    causal = (_iota2((nh * C, C), 0) & (C - 1)) >= _iota2((nh * C, C), 1)
    own_head = (_iota2((nh * C, kd), 0) >> sh_c) == (_iota2((nh * C, kd), 1) >> sh_k)
    eye = (_iota2((kd, kd), 0) == _iota2((kd, kd), 1)).astype(BF16)
    sub = _iota2((BF16_SUBLANES, kd), 0)
    scale = hk ** -0.5

    for c in range(tile // C):
        rows = pl.ds(c * C, C)
        t = i * tile + c * C + _iota2((C, 1), 0)
        g = jnp.where(t >= n_pad, la_ref[rows, :], 0.0)
        g_hi, g_lo = _split_bf16(g)
        bcum = _dot(tri, g_hi) + _dot(tri, g_lo)
        b_last = bcum[C - 1:C, :]
        q_dec = (q_ref[rows, :].astype(F32) * scale * jnp.exp(bcum)).astype(BF16)
        kf = k_ref[rows, :].astype(F32)
        k_intra = (kf * jnp.exp(-bcum)).astype(BF16)
        k_state = kf * jnp.exp(b_last - bcum)
        v = v_ref[rows, :]
        s_old = s_ref[...]
        q_heads = jnp.where(own_head, jnp.concatenate([q_dec] * nh, axis=0), jnp.zeros((), BF16))
        scores = jnp.where(causal, _dot_tr(q_heads, k_intra), 0.0).astype(BF16)
        o_inter = _dot(q_heads, s_old.astype(BF16))
        bl_hi = b_last.astype(BF16).astype(F32)
        tail = jnp.where(sub == 0, bl_hi, jnp.where(sub == 1, b_last - bl_hi, 0.0))
        moved = _dot_tr(eye, jnp.concatenate([k_state, tail], axis=0).astype(BF16))
        k_t = moved[:, :C].astype(BF16)
        decay = jnp.exp(moved[:, C:C + 1] + moved[:, C + 1:C + 2])
        kv = []
        for h in range(nh):
            rs = slice(h * C, (h + 1) * C)
            vs = slice(h * hv, (h + 1) * hv)
            o = _dot(scores[rs, :], v[:, vs]) + o_inter[rs, :]
            on = _rms(o, gain_ref[:, vs])
            rg = r_ref[rows, vs].astype(F32)
            o_ref[rows, vs] = (rg * _sigmoid(rg) * on).astype(o_ref.dtype)
            kv.append(_dot(k_t[h * hk:(h + 1) * hk, :], v[:, vs]))
        s_ref[...] = s_old * decay + jnp.concatenate(kv, axis=0)


def _gla(l, q, k, v, la, r, gain, bsz, t_pad, n_pad, tile):
    n, kd = q.shape
    vd = v.shape[1]
    tiles = t_pad // tile
    hk, hv = kd // GLA_HEADS, vd // GLA_HEADS
    row = lambda w: pl.BlockSpec((tile, w), lambda b, i: (b * tiles + i, 0))
    return pl.pallas_call(
        functools.partial(_gla_kernel, n_pad=n_pad, tile=tile, hk=hk, hv=hv),
        grid=(bsz, tiles),
        in_specs=[row(kd), row(kd), row(vd), row(kd), row(vd), _layer_spec(gain, l)],
        out_specs=row(vd),
        out_shape=jax.ShapeDtypeStruct((n, vd), BF16),
        scratch_shapes=[pltpu.VMEM((kd, hv), F32)],
        compiler_params=_cparams("parallel", "arbitrary"),
        name="gla",
    )(q, k, v, la, r, gain)


def _merge_ffn_kernel(h_ref, ys_ref, gl_ref, ga_ref, gb_ref, wglu_ref, bglu_ref, wpa_ref, wpb_ref,
                      wout_ref, g2_ref, w1_ref, w3_ref, w2_ref, gf_ref, o_ref, y_s, *ring,
                      ff_chunk, header, blocks_per_seq):
    L = SSM_CHUNK
    final = bool(ring)
    if final:
        obuf, sem = ring
        i, n_tiles = pl.program_id(0), pl.num_programs(0)
        nb = h_ref.shape[0] // header

        def for_real_blocks(tile, slot, action):
            for k in range(nb):
                gb = tile * nb + k
                seq_id = gb // blocks_per_seq
                blk = gb - seq_id * blocks_per_seq

                @pl.when(blk >= 1)
                def _(k=k, seq_id=seq_id, blk=blk):
                    dst = (seq_id * (blocks_per_seq - 1) + blk - 1) * header
                    action(pltpu.make_async_copy(obuf.at[slot, pl.ds(k * header, header), :],
                                                 o_ref.at[pl.ds(dst, header), :], sem.at[slot]))

        @pl.when(i >= 2)
        def _():
            for_real_blocks(i - 2, i % 2, lambda c: c.wait())

    nj, rows = ys_ref.shape[0], ys_ref.shape[1]
    for j in range(nj):
        for t in range(L):
            y_s[j, pl.ds(t, rows, stride=L), :] = ys_ref[j, :, t * LANES:(t + 1) * LANES].astype(F32)
    ys = jnp.concatenate([y_s[j] for j in range(nj)], axis=1)
    act = _gelu_tanh(ys)
    gate = _dot(act.astype(BF16), wglu_ref[...]) + bglu_ref[...]
    s5o = act * _sigmoid(gate)
    y_a = _dot(s5o.astype(BF16), wpa_ref[...])
    y_b = _dot(gl_ref[...], wpb_ref[...])
    mixed = _sigmoid(ga_ref[...].astype(F32)) * y_a + _sigmoid(gb_ref[...].astype(F32)) * y_b
    h1 = h_ref[...] + _dot(mixed.astype(BF16), wout_ref[...])
    z2 = _rms(h1, g2_ref[...]).astype(BF16)
    acc = h1
    d_ff = w1_ref.shape[1]
    for j in range(0, d_ff, ff_chunk):
        a = _dot(z2, w1_ref[:, j:j + ff_chunk])
        b = _dot(z2, w3_ref[:, j:j + ff_chunk])
        hid = (a * _sigmoid(a) * b).astype(BF16)
        acc = acc + _dot(hid, w2_ref[j:j + ff_chunk, :])
    if not final:
        o_ref[...] = acc
        return
    obuf[i % 2] = _rms(acc, gf_ref[...])
    for_real_blocks(i, i % 2, lambda c: c.start())

    @pl.when(i == n_tiles - 1)
    def _():
        @pl.when(i >= 1)
        def _():
            for_real_blocks(i - 1, (i + 1) % 2, lambda c: c.wait())
        for_real_blocks(i, i % 2, lambda c: c.wait())


def _merge_ffn(l, h, ys, gl, ga, gb, wglu, bglu, wpa, wpb, wout, g2, w1, w3, w2, gf, tm, ff_chunk,
               final_rows=None, header=None, blocks_per_seq=None):
    n, d = h.shape
    L = SSM_CHUNK
    nj = ys.shape[0]
    final = final_rows is not None
    row = lambda w: pl.BlockSpec((tm, w), lambda i: (i, 0))
    consts = (wglu, bglu, wpa, wpb, wout, g2, w1, w3, w2, gf)
    scratch = [pltpu.VMEM((nj, tm, LANES), F32)]
    if final:
        assert tm % header == 0
        scratch += [pltpu.VMEM((2, tm, d), F32), pltpu.SemaphoreType.DMA((2,))]
    return pl.pallas_call(
        functools.partial(_merge_ffn_kernel, ff_chunk=ff_chunk, header=header, blocks_per_seq=blocks_per_seq),
        grid=(n // tm,),
        in_specs=[row(d), pl.BlockSpec((nj, tm // L, L * LANES), lambda i: (0, i, 0)),
                  row(gl.shape[1]), row(d), row(d)]
                 + [_layer_spec(c, l) for c in consts[:-1]] + [_layer_spec(gf, 0)],
        out_specs=pl.BlockSpec(memory_space=pl.ANY) if final else row(d),
        out_shape=jax.ShapeDtypeStruct((final_rows if final else n, d), F32),
        scratch_shapes=scratch,
        compiler_params=_cparams("arbitrary" if final else "parallel"),
        name="merge_ffn",
    )(h, ys, gl, ga, gb, *consts)


def _largest_divisor(n, cap, mult):
    best = mult
    for t in range(mult, cap + 1, mult):
        if n % t == 0:
            best = t
    return best


def kernel(x, meta, norm1, w_in, lam_re, lam_im, log_step, b_re, b_im, c_re, c_im, d_skip, w_glu, b_glu, w_pa, w_alpha, b_alpha, gla_norm, w_pb, w_out, norm2, w_ff1, w_ff3, w_ff2, norm_f):
    bsz, seq, d = x.shape
    depth = w_in.shape[0]
    G = lam_re.shape[1]
    sw = G * SSM_GROUP
    kd = w_alpha.shape[2]
    vd = w_pb.shape[1]
    d_ff = w_ff1.shape[2]
    L = SSM_CHUNK
    n_pad = (-(N_META + seq)) % GLA_CHUNK
    t_pad = n_pad + N_META + seq
    n = bsz * t_pad
    n_chunks = t_pad // L
    nj = sw // LANES
    header = n_pad + N_META
    assert bsz % 8 == 0 and t_pad % L == 0 and G % SLAB_GROUPS == 0 and seq % header == 0

    tm = _largest_divisor(n, 512, L * BF16_SUBLANES)
    tm_in = _largest_divisor(n, 1280, L * BF16_SUBLANES)
    gla_tile = _largest_divisor(t_pad, 1024, GLA_CHUNK)
    ff_chunk = _largest_divisor(d_ff, MXU_TILE, LANES)
    s5_rows = _largest_divisor(bsz * n_chunks, 1040, BF16_SUBLANES)

    widths = (kd, kd, vd, vd, d, d)
    lo = sw + 2 * kd + 2 * vd
    w_main, w_gates, w_al = _w_in_split(w_in, lo, GLA_GATE_RANK, _largest_divisor(d, 256, BF16_SUBLANES))
    w_alpha_p = jnp.pad(w_alpha, ((0, 0), (0, LANES - GLA_GATE_RANK), (0, 0))).astype(BF16)
    bf = lambda a: a.astype(BF16)
    w_glu_b, w_pa_b, w_pb_b, w_out_b = bf(w_glu), bf(w_pa), bf(w_pb), bf(w_out)
    w1_b, w3_b, w2_b = bf(w_ff1), bf(w_ff3), bf(w_ff2)

    tstack, win, wout, adv = _s5_prep(lam_re, lam_im, log_step, b_re, b_im, c_re, c_im)
    dflat = jnp.tile(d_skip.reshape(depth, nj, 1, LANES), (1, 1, 1, L))

    meta_b = jnp.broadcast_to(meta.astype(x.dtype)[None], (bsz, N_META, d))
    h = jnp.concatenate([jnp.zeros((bsz, n_pad, d), x.dtype), meta_b, x], axis=1).reshape(n, d)

    vec = lambda a: a[:, None, :]
    for l in range(depth):
        u_rows, q, k, v, r, ga, gb, la = _inproj(
            l, h, vec(norm1), w_main, w_gates, w_al, w_alpha_p, vec(b_alpha), sw, widths, tm_in)
        x_rows = _s5_state(l, u_rows, win, adv, n_chunks, bsz)
        y_rows = _s5_out(l, u_rows, x_rows, tstack, wout, dflat, s5_rows)
        gl = _gla(l, q, k, v, la, r, vec(gla_norm), bsz, t_pad, n_pad, gla_tile)
        last = l == depth - 1
        h = _merge_ffn(l, h, y_rows, gl, ga, gb, w_glu_b, vec(b_glu), w_pa_b, w_pb_b, w_out_b,
                       vec(norm2), w1_b, w3_b, w2_b, norm_f[None, None, :], tm, ff_chunk,
                       **(dict(final_rows=bsz * seq, header=header, blocks_per_seq=t_pad // header) if last else {}))
    return h.reshape(bsz, seq, d)
```

```python
import functools
import math

import jax
import jax.numpy as jnp
from jax import lax
from jax.experimental import pallas as pl
from jax.experimental.pallas import tpu as pltpu

F32 = jnp.float32
BF16 = jnp.bfloat16

EPS = 1e-6
N_META = 16
SSM_GROUP = 16
SSM_CHUNK = 16
GLA_HEADS = 4
GLA_CHUNK = 64
GLA_GATE_RANK = 16
GLA_GATE_TAU = 16.0
LANES = 128
MXU_TILE = 256
BF16_SUBLANES = 16
SLAB_GROUPS = LANES // SSM_GROUP
VMEM_LIMIT = 56 * 1024 * 1024


def _cparams(*sem):
    return pltpu.CompilerParams(dimension_semantics=sem, vmem_limit_bytes=VMEM_LIMIT)


def _layer_spec(a, l):
    nd = a.ndim
    return pl.BlockSpec((None,) + a.shape[1:], lambda *_: (l,) + (0,) * (nd - 1), pipeline_mode=pl.Buffered(1))


def _layer_slab_spec(a, l):
    nd = a.ndim
    return pl.BlockSpec((None, 1) + a.shape[2:], lambda j, *_: (l, j) + (0,) * (nd - 2))


def _lead_spec(a):
    return pl.BlockSpec((1,) + a.shape[1:], lambda i, *_: (i,) + (0,) * (a.ndim - 1))


def _rms(x, gain):
    ms = jnp.mean(x * x, axis=-1, keepdims=True)
    return x * lax.rsqrt(ms + EPS) * gain


def _sigmoid(x):
    return 1.0 / (1.0 + jnp.exp(-x))


def _gelu_tanh(x):
    c = math.sqrt(2.0 / math.pi)
    return x * (0.5 * (1.0 + jnp.tanh(c * (x + 0.044715 * (x * x * x)))))


def _dot(a, b):
    return jnp.dot(a, b, preferred_element_type=F32)


def _dot_tl(a, b):
    return lax.dot_general(a, b, (((0,), (0,)), ((), ())), preferred_element_type=F32)


def _dot_tr(a, b):
    return lax.dot_general(a, b, (((1,), (1,)), ((), ())), preferred_element_type=F32)


def _split_bf16(x):
    hi = x.astype(BF16)
    lo = (x - hi.astype(F32)).astype(BF16)
    return hi, lo


def _iota2(shape, axis):
    return lax.broadcasted_iota(jnp.int32, shape, axis)


def _lam_bar(lr, li, step):
    mag = jnp.exp(lr * step)
    return mag * jnp.cos(li * step), mag * jnp.sin(li * step)


def _cmul(ar, ai, br, bi):
    return ar * br - ai * bi, ar * bi + ai * br


def _dot_tr3(a, b):
    (a_hi, a_lo), (b_hi, b_lo) = _split_bf16(a), _split_bf16(b)
    return _dot_tr(a_hi, b_hi) + _dot_tr(a_hi, b_lo) + _dot_tr(a_lo, b_hi)


def _s5_prep_kernel(rows_ref, lr_ref, li_ref, ls_ref, btr_ref, bti_ref, cr_ref, ci_ref,
                    ts_ref, wi_ref, wo_ref, adv_ref):
    L, H = SSM_CHUNK, SSM_GROUP
    GH, P = cr_ref.shape[2], cr_ref.shape[3]
    GP = rows_ref.shape[3]
    sh_h, sh_p = H.bit_length() - 1, P.bit_length() - 1
    step = jnp.exp(ls_ref[0, 0])
    lr = jnp.minimum(lr_ref[0, 0], -1e-4)
    li = li_ref[0, 0]
    ab_re, ab_im = _lam_bar(lr, li, step)
    den = lr * lr + li * li
    nr = ab_re - 1.0
    coef_re = (nr * lr + ab_im * li) / den
    coef_im = (ab_im * lr - nr * li) / den
    bt_re, bt_im = btr_ref[0, 0], bti_ref[0, 0]
    bb_re = coef_re * bt_re - coef_im * bt_im
    bb_im = coef_re * bt_im + coef_im * bt_re
    c_re, c_im = cr_ref[0, 0], ci_ref[0, 0]
    ca_re, ca_im = [c_re], [c_im]
    for _ in range(L):
        nxt = _cmul(ca_re[-1], ca_im[-1], ab_re, ab_im)
        ca_re.append(nxt[0])
        ca_im.append(nxt[1])

    k_all = (_dot_tr3(bb_re, jnp.concatenate(ca_re[:L], axis=0))
             - _dot_tr3(bb_im, jnp.concatenate(ca_im[:L], axis=0)))
    same = (_iota2(k_all.shape, 0) >> sh_h) == ((_iota2(k_all.shape, 1) & (GH - 1)) >> sh_h)
    k_all = jnp.where(same, k_all, 0.0).astype(ts_ref.dtype)
    k_lag = lambda lag: k_all[:, lag * GH:(lag + 1) * GH]
    nt = L // 2
    for d in range(nt):
        r0 = (nt - 1 - d) * 2 * GH
        ts_ref[0, 0, r0:r0 + GH, 0:GH] = k_lag(2 * d)
        ts_ref[0, 0, r0:r0 + GH, GH:2 * GH] = k_lag(2 * d + 1)
        ts_ref[0, 0, r0 + GH:r0 + 2 * GH, 0:GH] = k_lag(2 * d - 1) if d else jnp.zeros((GH, GH), ts_ref.dtype)
        ts_ref[0, 0, r0 + GH:r0 + 2 * GH, GH:2 * GH] = k_lag(2 * d)

    sel_t = ((_iota2((GP, P), 0) & (P - 1)) == _iota2((GP, P), 1)).astype(BF16)
    move_t = lambda x: sum(_dot_tr(sel_t, part) for part in _split_bf16(x))
    w_re = move_t(jnp.concatenate(ca_re[1:], axis=0))
    w_im = move_t(jnp.concatenate(ca_im[1:], axis=0))
    same = (_iota2(w_re.shape, 0) >> sh_p) == ((_iota2(w_re.shape, 1) & (GH - 1)) >> sh_h)
    wo_ref[0, 0, 0:GP, :] = jnp.where(same, w_re, 0.0).astype(wo_ref.dtype)
    wo_ref[0, 0, GP:2 * GP, :] = jnp.where(same, -w_im, 0.0).astype(wo_ref.dtype)

    sel = ((_iota2((P, GP), 1) & (P - 1)) == _iota2((P, GP), 0)).astype(BF16)
    tile_lanes = lambda x: sum(_dot(part, sel) for part in _split_bf16(x))
    same = (_iota2((GH, GP), 0) >> sh_h) == (_iota2((GH, GP), 1) >> sh_p)
    bbt_re = jnp.where(same, tile_lanes(bb_re), 0.0)
    bbt_im = jnp.where(same, tile_lanes(bb_im), 0.0)
    lr_gp = jnp.minimum(rows_ref[0, 0, 0:1, :], -1e-4)
    a_re, a_im = _lam_bar(lr_gp, rows_ref[0, 0, 1:2, :], jnp.exp(rows_ref[0, 0, 2:3, :]))
    pw = [(jnp.ones_like(a_re), jnp.zeros_like(a_re))]
    for _ in range(L):
        pw.append(_cmul(pw[-1][0], pw[-1][1], a_re, a_im))
    for s in range(L):
        ar, ai = pw[L - 1 - s]
        wi_ref[0, 0, s * GH:(s + 1) * GH, 0:GP] = (bbt_re * ar - bbt_im * ai).astype(wi_ref.dtype)
        wi_ref[0, 0, s * GH:(s + 1) * GH, GP:2 * GP] = (bbt_re * ai + bbt_im * ar).astype(wi_ref.dtype)
    adv_ref[0, 0, 0:1, :] = pw[L][0]
    adv_ref[0, 0, 1:2, :] = pw[L][1]


def _s5_prep(lam_re, lam_im, log_step, b_re, b_im, c_re, c_im):
    nl, G, P = lam_re.shape
    H = b_re.shape[-1]
    L, S = SSM_CHUNK, SLAB_GROUPS
    J = G // S
    GH, GP = S * H, S * P
    per_h = lambda a: jnp.repeat(a[:, :, None, :], H, axis=2).reshape(nl, J, GH, a.shape[-1])
    rows = jnp.stack([lam_re.reshape(nl, J, GP), lam_im.reshape(nl, J, GP),
                      jnp.repeat(log_step, P, axis=-1).reshape(nl, J, GP)], axis=2)
    ins = (rows, per_h(lam_re), per_h(lam_im), per_h(log_step[..., None]),
           b_re.transpose(0, 1, 3, 2).reshape(nl, J, GH, P), b_im.transpose(0, 1, 3, 2).reshape(nl, J, GH, P),
           c_re.reshape(nl, J, GH, P), c_im.reshape(nl, J, GH, P))
    outs = ((nl, J, L * GH, 2 * GH), (nl, J, L * GH, 2 * GP), (nl, J, 2 * GP, L * GH), (nl, J, 2, GP))
    dts = (BF16, BF16, BF16, F32)
    blk = lambda shape: pl.BlockSpec((1, 1) + shape[2:], lambda i, j: (i, j, 0, 0))
    return pl.pallas_call(
        _s5_prep_kernel,
        grid=(nl, J),
        in_specs=[blk(a.shape) for a in ins],
        out_specs=[blk(o) for o in outs],
        out_shape=[jax.ShapeDtypeStruct(o, dt) for o, dt in zip(outs, dts)],
        compiler_params=_cparams("parallel", "parallel"),
        name="s5_prep",
    )(*ins)


def _w_in_split_kernel(w_ref, main_ref, gates_ref, al_ref, *, lo, rank):
    w = w_ref[0]
    main_ref[0] = w[:, :lo].astype(main_ref.dtype)
    gates_ref[0] = w[:, lo + rank:].astype(gates_ref.dtype)
    al_ref[0] = jnp.zeros(al_ref.shape[1:], al_ref.dtype)
    al_ref[0, :, :rank] = w[:, lo:lo + rank].astype(al_ref.dtype)


def _w_in_split(w_in, lo, rank, rows):
    nl, d, cols = w_in.shape
    hi = cols - lo - rank
    blk = lambda w: pl.BlockSpec((1, rows, w), lambda l, i: (l, i, 0))
    return pl.pallas_call(
        functools.partial(_w_in_split_kernel, lo=lo, rank=rank),
        grid=(nl, d // rows),
        in_specs=[blk(cols)],
        out_specs=[blk(lo), blk(hi), blk(LANES)],
        out_shape=[jax.ShapeDtypeStruct((nl, d, lo), BF16), jax.ShapeDtypeStruct((nl, d, hi), BF16),
                   jax.ShapeDtypeStruct((nl, d, LANES), BF16)],
        compiler_params=_cparams("parallel", "parallel"),
        name="w_in_split",
    )(w_in)


def _inproj_kernel(h_ref, gain_ref, w_ref, wg_ref, wal_ref, walpha_ref, balpha_ref, u_ref, *rest, widths):
    out_refs, u_s = rest[:-1], rest[-1]
    la_ref = out_refs[-1]
    n_lo = len(widths) - 2
    L = SSM_CHUNK
    zb = _rms(h_ref[...], gain_ref[...]).astype(BF16)
    a_low = _dot(zb, wal_ref[...])
    logits = _dot(a_low.astype(BF16), walpha_ref[...]) + balpha_ref[...]
    log_sig = jnp.minimum(logits, 0.0) - jnp.log(1.0 + jnp.exp(-jnp.abs(logits)))
    la_ref[...] = log_sig * (1.0 / GLA_GATE_TAU)
    nj = u_ref.shape[0]
    u = _dot(zb, w_ref[:, :nj * LANES])
    for j in range(nj):
        u_s[j] = u[:, j * LANES:(j + 1) * LANES]
    rows = u_ref.shape[1]
    for j in range(nj):
        for t in range(L):
            u_ref[j, :, t * LANES:(t + 1) * LANES] = u_s[j, pl.ds(t, rows, stride=L), :].astype(u_ref.dtype)
    start = nj * LANES
    for ref, width in zip(out_refs[:n_lo], widths[:n_lo]):
        ref[...] = _dot(zb, w_ref[:, start:start + width]).astype(ref.dtype)
        start += width
    start = 0
    for ref, width in zip(out_refs[n_lo:-1], widths[n_lo:]):
        ref[...] = _dot(zb, wg_ref[:, start:start + width]).astype(ref.dtype)
        start += width


def _inproj(l, h, gain, w_main, w_gates, w_al, w_alpha, b_alpha, sw, widths, tm):
    n, d = h.shape
    kd = w_alpha.shape[-1]
    L = SSM_CHUNK
    row = lambda w: pl.BlockSpec((tm, w), lambda i: (i, 0))
    nj = sw // LANES
    return pl.pallas_call(
        functools.partial(_inproj_kernel, widths=widths),
        grid=(n // tm,),
        in_specs=[row(d)] + [_layer_spec(a, l) for a in (gain, w_main, w_gates, w_al, w_alpha, b_alpha)],
        out_specs=[pl.BlockSpec((nj, tm // L, L * LANES), lambda i: (0, i, 0))]
                  + [row(w) for w in widths] + [row(kd)],
        out_shape=[jax.ShapeDtypeStruct((nj, n // L, L * LANES), BF16)]
                  + [jax.ShapeDtypeStruct((n, w), BF16) for w in widths]
                  + [jax.ShapeDtypeStruct((n, kd), F32)],
        scratch_shapes=[pltpu.VMEM((nj, tm, LANES), F32)],
        compiler_params=_cparams("parallel"),
        name="inproj",
    )(h, gain, w_main, w_gates, w_al, w_alpha, b_alpha)


def _s5_state_kernel(u_ref, win_ref, adv_ref, x_ref, v_s, *, n_chunks, bsz):
    v = _dot(u_ref[0], win_ref[0])
    nv = v_s.shape[0]
    nc = nv // 2
    for p in range(nv):
        v_s[p] = v[:, p * LANES:(p + 1) * LANES]
    a_re = [jnp.broadcast_to(adv_ref[0, 0:1, p * LANES:(p + 1) * LANES], (bsz, LANES)) for p in range(nc)]
    a_im = [jnp.broadcast_to(adv_ref[0, 1:2, p * LANES:(p + 1) * LANES], (bsz, LANES)) for p in range(nc)]

    def step(c, carry):
        rows = pl.ds(c, bsz, stride=n_chunks)
        new = []
        for p in range(nc):
            xr, xi = carry[p], carry[nc + p]
            vr = v_s[p, rows, :]
            vi = v_s[nc + p, rows, :]
            v_s[p, rows, :] = xr
            v_s[nc + p, rows, :] = xi
            new.append((a_re[p] * xr - a_im[p] * xi + vr, a_re[p] * xi + a_im[p] * xr + vi))
        return tuple(n[0] for n in new) + tuple(n[1] for n in new)

    zero = jnp.zeros((bsz, LANES), F32)
    lax.fori_loop(0, n_chunks, step, (zero,) * nv)
    for p in range(nv):
        x_ref[0, :, p * LANES:(p + 1) * LANES] = v_s[p].astype(x_ref.dtype)


def _s5_state(l, u_rows, win, adv, n_chunks, bsz):
    nj, rows, _ = u_rows.shape
    sw2 = win.shape[-1]
    return pl.pallas_call(
        functools.partial(_s5_state_kernel, n_chunks=n_chunks, bsz=bsz),
        grid=(nj,),
        in_specs=[_lead_spec(u_rows), _layer_slab_spec(win, l), _layer_slab_spec(adv, l)],
        out_specs=pl.BlockSpec((1, rows, sw2), lambda i: (i, 0, 0)),
        out_shape=jax.ShapeDtypeStruct((nj, rows, sw2), BF16),
        scratch_shapes=[pltpu.VMEM((sw2 // LANES, rows, LANES), F32)],
        compiler_params=_cparams("parallel"),
        name="s5_state",
    )(u_rows, win, adv)


def _s5_out_kernel(u_ref, x_ref, ts_ref, wo_ref, d_ref, y_ref):
    u = u_ref[0]
    x = x_ref[0]
    nt = u.shape[1] // MXU_TILE
    for b in range(nt):
        cols = slice(b * MXU_TILE, (b + 1) * MXU_TILE)
        y = _dot(u[:, :(b + 1) * MXU_TILE], ts_ref[0, (nt - 1 - b) * MXU_TILE:, :])
        y = y + _dot(x, wo_ref[0, :, cols])
        y = y + u[:, cols].astype(F32) * d_ref[0, :, cols]
        y_ref[0, :, cols] = y.astype(y_ref.dtype)


def _s5_out(l, u_rows, x_rows, tstack, wout, dflat, tr):
    nj, rows, width = u_rows.shape
    sw2 = x_rows.shape[-1]
    return pl.pallas_call(
        _s5_out_kernel,
        grid=(nj, rows // tr),
        in_specs=[pl.BlockSpec((1, tr, width), lambda j, i: (j, i, 0)),
                  pl.BlockSpec((1, tr, sw2), lambda j, i: (j, i, 0)),
                  _layer_slab_spec(tstack, l), _layer_slab_spec(wout, l), _layer_slab_spec(dflat, l)],
        out_specs=pl.BlockSpec((1, tr, width), lambda j, i: (j, i, 0)),
        out_shape=jax.ShapeDtypeStruct(u_rows.shape, BF16),
        compiler_params=_cparams("parallel", "parallel"),
        name="s5_out",
    )(u_rows, x_rows, tstack, wout, dflat)


def _gla_kernel(q_ref, k_ref, v_ref, la_ref, r_ref, gain_ref, o_ref, s_ref, *, n_pad, tile, hk, hv):
    i = pl.program_id(1)
    C = GLA_CHUNK

    @pl.when(i == 0)
    def _():
        s_ref[...] = jnp.zeros_like(s_ref)

    nh = GLA_HEADS
    kd = nh * hk
    sh_c, sh_k = C.bit_length() - 1, hk.bit_length() - 1
    tri = (_iota2((C, C), 0) >= _iota2((C, C), 1)).astype(BF16)
    causal = (_iota2((nh * C, C), 0) & (C - 1)) >= _iota2((nh * C, C), 1)
    own_head = (_iota2((nh * C, kd), 0) >> sh_c) == (_iota2((nh * C, kd), 1) >> sh_k)
    eye = (_iota2((kd, kd), 0) == _iota2((kd, kd), 1)).astype(BF16)
    sub = _iota2((BF16_SUBLANES, kd), 0)
    scale = hk ** -0.5

    for c in range(tile // C):
        rows = pl.ds(c * C, C)
        t = i * tile + c * C + _iota2((C, 1), 0)
        g = jnp.where(t >= n_pad, la_ref[rows, :], 0.0)
        g_hi, g_lo = _split_bf16(g)
        bcum = _dot(tri, g_hi) + _dot(tri, g_lo)
        b_last = bcum[C - 1:C, :]
        q_dec = (q_ref[rows, :].astype(F32) * scale * jnp.exp(bcum)).astype(BF16)
        kf = k_ref[rows, :].astype(F32)
        k_intra = (kf * jnp.exp(-bcum)).astype(BF16)
        k_state = kf * jnp.exp(b_last - bcum)
        v = v_ref[rows, :]
        s_old = s_ref[...]
        q_heads = jnp.where(own_head, jnp.concatenate([q_dec] * nh, axis=0), jnp.zeros((), BF16))
        scores = jnp.where(causal, _dot_tr(q_heads, k_intra), 0.0).astype(BF16)
        o_inter = _dot(q_heads, s_old.astype(BF16))
        bl_hi = b_last.astype(BF16).astype(F32)
        tail = jnp.where(sub == 0, bl_hi, jnp.where(sub == 1, b_last - bl_hi, 0.0))
        moved = _dot_tr(eye, jnp.concatenate([k_state, tail], axis=0).astype(BF16))
        k_t = moved[:, :C].astype(BF16)
        decay = jnp.exp(moved[:, C:C + 1] + moved[:, C + 1:C + 2])
        kv = []
        for h in range(nh):
            rs = slice(h * C, (h + 1) * C)
            vs = slice(h * hv, (h + 1) * hv)
            o = _dot(scores[rs, :], v[:, vs]) + o_inter[rs, :]
            on = _rms(o, gain_ref[:, vs])
            rg = r_ref[rows, vs].astype(F32)
            o_ref[rows, vs] = (rg * _sigmoid(rg) * on).astype(o_ref.dtype)
            kv.append(_dot(k_t[h * hk:(h + 1) * hk, :], v[:, vs]))
        s_ref[...] = s_old * decay + jnp.concatenate(kv, axis=0)


def _gla(l, q, k, v, la, r, gain, bsz, t_pad, n_pad, tile):
    n, kd = q.shape
    vd = v.shape[1]
    tiles = t_pad // tile
    hk, hv = kd // GLA_HEADS, vd // GLA_HEADS
    row = lambda w: pl.BlockSpec((tile, w), lambda b, i: (b * tiles + i, 0))
    return pl.pallas_call(
        functools.partial(_gla_kernel, n_pad=n_pad, tile=tile, hk=hk, hv=hv),
        grid=(bsz, tiles),
        in_specs=[row(kd), row(kd), row(vd), row(kd), row(vd), _layer_spec(gain, l)],
        out_specs=row(vd),
        out_shape=jax.ShapeDtypeStruct((n, vd), BF16),
        scratch_shapes=[pltpu.VMEM((kd, hv), F32)],
        compiler_params=_cparams("parallel", "arbitrary"),
        name="gla",
    )(q, k, v, la, r, gain)


def _merge_ffn_kernel(h_ref, ys_ref, gl_ref, ga_ref, gb_ref, wglu_ref, bglu_ref, wpa_ref, wpb_ref,
                      wout_ref, g2_ref, w1_ref, w3_ref, w2_ref, gf_ref, o_ref, y_s, *ring,
                      ff_chunk, header, blocks_per_seq):
    L = SSM_CHUNK
    final = bool(ring)
    if final:
        obuf, sem = ring
        i, n_tiles = pl.program_id(0), pl.num_programs(0)
        nb = h_ref.shape[0] // header

        def for_real_blocks(tile, slot, action):
            for k in range(nb):
                gb = tile * nb + k
                seq_id = gb // blocks_per_seq
                blk = gb - seq_id * blocks_per_seq

                @pl.when(blk >= 1)
                def _(k=k, seq_id=seq_id, blk=blk):
                    dst = (seq_id * (blocks_per_seq - 1) + blk - 1) * header
                    action(pltpu.make_async_copy(obuf.at[slot, pl.ds(k * header, header), :],
                                                 o_ref.at[pl.ds(dst, header), :], sem.at[slot]))

        @pl.when(i >= 2)
        def _():
            for_real_blocks(i - 2, i % 2, lambda c: c.wait())

    nj, rows = ys_ref.shape[0], ys_ref.shape[1]
    for j in range(nj):
        for t in range(L):
            y_s[j, pl.ds(t, rows, stride=L), :] = ys_ref[j, :, t * LANES:(t + 1) * LANES].astype(F32)
    ys = jnp.concatenate([y_s[j] for j in range(nj)], axis=1)
    act = _gelu_tanh(ys)
    gate = _dot(act.astype(BF16), wglu_ref[...]) + bglu_ref[...]
    s5o = act * _sigmoid(gate)
    y_a = _dot(s5o.astype(BF16), wpa_ref[...])
    y_b = _dot(gl_ref[...], wpb_ref[...])
    mixed = _sigmoid(ga_ref[...].astype(F32)) * y_a + _sigmoid(gb_ref[...].astype(F32)) * y_b
    h1 = h_ref[...] + _dot(mixed.astype(BF16), wout_ref[...])
    z2 = _rms(h1, g2_ref[...]).astype(BF16)
    acc = h1
    d_ff = w1_ref.shape[1]
    for j in range(0, d_ff, ff_chunk):
        a = _dot(z2, w1_ref[:, j:j + ff_chunk])
        b = _dot(z2, w3_ref[:, j:j + ff_chunk])
        hid = (a * _sigmoid(a) * b).astype(BF16)
        acc = acc + _dot(hid, w2_ref[j:j + ff_chunk, :])
    if not final:
        o_ref[...] = acc
        return
    obuf[i % 2] = _rms(acc, gf_ref[...])
    for_real_blocks(i, i % 2, lambda c: c.start())

    @pl.when(i == n_tiles - 1)
    def _():
        @pl.when(i >= 1)
        def _():
            for_real_blocks(i - 1, (i + 1) % 2, lambda c: c.wait())
        for_real_blocks(i, i % 2, lambda c: c.wait())


def _merge_ffn(l, h, ys, gl, ga, gb, wglu, bglu, wpa, wpb, wout, g2, w1, w3, w2, gf, tm, ff_chunk,
               final_rows=None, header=None, blocks_per_seq=None):
    n, d = h.shape
    L = SSM_CHUNK
    nj = ys.shape[0]
    final = final_rows is not None
    row = lambda w: pl.BlockSpec((tm, w), lambda i: (i, 0))
    consts = (wglu, bglu, wpa, wpb, wout, g2, w1, w3, w2, gf)
    scratch = [pltpu.VMEM((nj, tm, LANES), F32)]
    if final:
        assert tm % header == 0
        scratch += [pltpu.VMEM((2, tm, d), F32), pltpu.SemaphoreType.DMA((2,))]
    return pl.pallas_call(
        functools.partial(_merge_ffn_kernel, ff_chunk=ff_chunk, header=header, blocks_per_seq=blocks_per_seq),
        grid=(n // tm,),
        in_specs=[row(d), pl.BlockSpec((nj, tm // L, L * LANES), lambda i: (0, i, 0)),
                  row(gl.shape[1]), row(d), row(d)]
                 + [_layer_spec(c, l) for c in consts[:-1]] + [_layer_spec(gf, 0)],
        out_specs=pl.BlockSpec(memory_space=pl.ANY) if final else row(d),
        out_shape=jax.ShapeDtypeStruct((final_rows if final else n, d), F32),
        scratch_shapes=scratch,
        compiler_params=_cparams("arbitrary" if final else "parallel"),
        name="merge_ffn",
    )(h, ys, gl, ga, gb, *consts)


def _largest_divisor(n, cap, mult):
    best = mult
    for t in range(mult, cap + 1, mult):
        if n % t == 0:
            best = t
    return best


def kernel(x, meta, norm1, w_in, lam_re, lam_im, log_step, b_re, b_im, c_re, c_im, d_skip, w_glu, b_glu, w_pa, w_alpha, b_alpha, gla_norm, w_pb, w_out, norm2, w_ff1, w_ff3, w_ff2, norm_f):
    bsz, seq, d = x.shape
    depth = w_in.shape[0]
    G = lam_re.shape[1]
    sw = G * SSM_GROUP
    kd = w_alpha.shape[2]
    vd = w_pb.shape[1]
    d_ff = w_ff1.shape[2]
    L = SSM_CHUNK
    n_pad = (-(N_META + seq)) % GLA_CHUNK
    t_pad = n_pad + N_META + seq
    n = bsz * t_pad
    n_chunks = t_pad // L
    nj = sw // LANES
    header = n_pad + N_META
    assert bsz % 8 == 0 and t_pad % L == 0 and G % SLAB_GROUPS == 0 and seq % header == 0

    tm = _largest_divisor(n, 512, L * BF16_SUBLANES)
    tm_in = _largest_divisor(n, 1280, L * BF16_SUBLANES)
    gla_tile = _largest_divisor(t_pad, 4224, GLA_CHUNK)
    ff_chunk = _largest_divisor(d_ff, MXU_TILE, LANES)
    s5_rows = _largest_divisor(bsz * n_chunks, 1040, BF16_SUBLANES)

    widths = (kd, kd, vd, vd, d, d)
    lo = sw + 2 * kd + 2 * vd
    w_main, w_gates, w_al = _w_in_split(w_in, lo, GLA_GATE_RANK, _largest_divisor(d, 256, BF16_SUBLANES))
    w_alpha_p = jnp.pad(w_alpha, ((0, 0), (0, LANES - GLA_GATE_RANK), (0, 0))).astype(BF16)
    bf = lambda a: a.astype(BF16)
    w_glu_b, w_pa_b, w_pb_b, w_out_b = bf(w_glu), bf(w_pa), bf(w_pb), bf(w_out)
    w1_b, w3_b, w2_b = bf(w_ff1), bf(w_ff3), bf(w_ff2)

    tstack, win, wout, adv = _s5_prep(lam_re, lam_im, log_step, b_re, b_im, c_re, c_im)
    dflat = jnp.tile(d_skip.reshape(depth, nj, 1, LANES), (1, 1, 1, L))

    meta_b = jnp.broadcast_to(meta.astype(x.dtype)[None], (bsz, N_META, d))
    h = jnp.concatenate([jnp.zeros((bsz, n_pad, d), x.dtype), meta_b, x], axis=1).reshape(n, d)

    vec = lambda a: a[:, None, :]
    for l in range(depth):
        u_rows, q, k, v, r, ga, gb, la = _inproj(
            l, h, vec(norm1), w_main, w_gates, w_al, w_alpha_p, vec(b_alpha), sw, widths, tm_in)
        x_rows = _s5_state(l, u_rows, win, adv, n_chunks, bsz)
        y_rows = _s5_out(l, u_rows, x_rows, tstack, wout, dflat, s5_rows)
        gl = _gla(l, q, k, v, la, r, vec(gla_norm), bsz, t_pad, n_pad, gla_tile)
        last = l == depth - 1
        h = _merge_ffn(l, h, y_rows, gl, ga, gb, w_glu_b, vec(b_glu), w_pa_b, w_pb_b, w_out_b,
                       vec(norm2), w1_b, w3_b, w2_b, norm_f[None, None, :], tm, ff_chunk,
                       **(dict(final_rows=bsz * seq, header=header, blocks_per_seq=t_pad // header) if last else {}))
    return h.reshape(bsz, seq, d)
```
